```python
import math
import jax, jax.numpy as jnp
from jax import lax
import numpy as np

D_MODEL = 1024
BATCH = 8
SEQ = 2048
DEPTH = 2

GRID_W = 64
CTX_LEN = 256
HEAD_DIM = 64
ROPE_THETA = 10000.0
Q_BLOCK = 128
EPS = 1e-6

B_WIDTH = D_MODEL // 4
B_GROUPS = 4
B_GROUP_DIM = B_WIDTH // B_GROUPS
POOL_WINDOWS = (2, 4, 8, 16)
A_WIDTH = D_MODEL - B_WIDTH
A_Q_HEADS = A_WIDTH // HEAD_DIM
A_KV_HEADS = A_Q_HEADS // 3
A_GROUP = A_Q_HEADS // A_KV_HEADS
A_KV_WIDTH = A_KV_HEADS * HEAD_DIM
A_IN_WIDTH = A_WIDTH + 2 * A_KV_WIDTH + B_WIDTH
D_CH = D_MODEL // 4
C_QK_DIM = HEAD_DIM
C_V_DIM = 2 * HEAD_DIM
C_WIDTH = D_MODEL - D_CH
C_HEADS = C_WIDTH // C_V_DIM
CONV_WIDTH = 31
C_IN_WIDTH = 3 * C_WIDTH + 2 * D_CH
FFN_HIDDEN = -(-8 * D_MODEL // (3 * 256)) * 256
ALPHA = (2.0 * DEPTH) ** 0.25
BETA = (8.0 * DEPTH) ** -0.25

F32 = jnp.float32

kernel_name = 'hybrid_diffusion_gqa_pool_diffattn_conformer'


def layer_norm(x, g, b):
    xf = x.astype(F32)
    mu = jnp.mean(xf, axis=-1, keepdims=True)
    var = jnp.mean(jnp.square(xf - mu), axis=-1, keepdims=True)
    return ((xf - mu) * lax.rsqrt(var + EPS) * g.astype(F32) + b.astype(F32)).astype(x.dtype)


def rms_norm(x, g):
    xf = x.astype(F32)
    return (xf * lax.rsqrt(jnp.mean(jnp.square(xf), axis=-1, keepdims=True) + EPS) * g.astype(F32)).astype(x.dtype)


def axial_rope_tables(n_tokens):
    rows = n_tokens // GRID_W
    row = jnp.repeat(jnp.arange(rows, dtype=F32), GRID_W)
    col = jnp.tile(jnp.arange(GRID_W, dtype=F32), rows)
    axis_dim = HEAD_DIM // 2
    freqs = ROPE_THETA ** (-jnp.arange(0, axis_dim, 2, dtype=F32) / axis_dim)
    ang = jnp.concatenate([row[:, None] * freqs, col[:, None] * freqs], axis=-1)
    return jnp.cos(ang), jnp.sin(ang)


def apply_axial_rope(x, cos, sin):
    q = HEAD_DIM // 4
    xf = x.astype(F32)

    def rot(xa, c, s):
        x1, x2 = xa[..., :q], xa[..., q:]
        return jnp.concatenate([x1 * c - x2 * s, x2 * c + x1 * s], axis=-1)

    out = jnp.concatenate([rot(xf[..., :2 * q], cos[:, :q], sin[:, :q]),
                           rot(xf[..., 2 * q:], cos[:, q:], sin[:, q:])], axis=-1)
    return out.astype(x.dtype)


def map_query_blocks(fn, q):
    t = q.shape[-2]
    nb = t // Q_BLOCK
    qb = jnp.moveaxis(q.reshape(q.shape[:-2] + (nb, Q_BLOCK, q.shape[-1])), -3, 0)
    out = jnp.moveaxis(lax.map(fn, qb), 0, -3)
    return out.reshape(out.shape[:-3] + (t, out.shape[-1]))


def gqa_attend(q, k, v):
    s = jnp.einsum('bkgqd,bksd->bkgqs', q, k, preferred_element_type=F32) * (HEAD_DIM ** -0.5)
    p = jax.nn.softmax(s, axis=-1)
    return jnp.einsum('bkgqs,bksd->bkgqd', p.astype(v.dtype), v)


def diff_attend(q, k, v, lam):
    s = jnp.einsum('bhiqd,bhisd->bhiqs', q, k, preferred_element_type=F32) * (C_QK_DIM ** -0.5)
    p = jax.nn.softmax(s, axis=-1)
    a = p[:, :, 0] - lam * p[:, :, 1]
    return jnp.einsum('bhqs,bhsd->bhqd', a.astype(v.dtype), v)


def centred_window_mean(x, w):
    t_len = x.shape[1]
    cs = jnp.pad(jnp.cumsum(x.astype(F32), axis=1), ((0, 0), (1, 0), (0, 0)))
    t = jnp.arange(t_len)
    lo = jnp.clip(t - w // 2, 0, t_len)
    hi = jnp.clip(t + (w - 1 - w // 2) + 1, 0, t_len)
    s = jnp.take(cs, hi, axis=1) - jnp.take(cs, lo, axis=1)
    return (s / (hi - lo).astype(F32)[None, :, None]).astype(x.dtype)


def pool_mixer(u, w_pool, pool_scale):
    bsz, n, _ = u.shape
    ug = u.reshape(bsz, n, B_GROUPS, B_GROUP_DIM)
    pooled = jnp.stack([centred_window_mean(ug[:, :, g], w) - ug[:, :, g]
                        for g, w in enumerate(POOL_WINDOWS)], axis=2)
    mixed = jnp.einsum('bngc,gcd->bngd', pooled, w_pool)
    return mixed.reshape(bsz, n, B_WIDTH) * pool_scale


def conformer_conv(u, conv_w, conv_b, ln_g, ln_b):
    a, g = jnp.split(u, 2, axis=-1)
    z = a * jax.nn.sigmoid(g)
    z = lax.conv_general_dilated(z, conv_w[:, None, :].astype(z.dtype), window_strides=(1,),
                                 padding=[(CONV_WIDTH // 2, CONV_WIDTH // 2)],
                                 dimension_numbers=('NWC', 'WIO', 'NWC'),
                                 feature_group_count=D_CH) + conv_b
    return jax.nn.silu(layer_norm(z, ln_g, ln_b))


def swiglu(h, w_in, w_out):
    a, g = jnp.split(h @ w_in, 2, axis=-1)
    return (jax.nn.silu(g) * a) @ w_out


def mixer_ab(h, hc, w_in, q_gain, k_gain, w_pool, pool_scale, w_out, cos, sin, with_ctx_out):
    bsz, n_lat, _ = h.shape
    n_ctx = hc.shape[1]
    n = n_ctx + n_lat
    p = jnp.concatenate([hc, h], axis=1) @ w_in
    q, k, v, u = jnp.split(p, [A_WIDTH, A_WIDTH + A_KV_WIDTH, A_WIDTH + 2 * A_KV_WIDTH], axis=-1)
    q = rms_norm(q.reshape(bsz, n, A_KV_HEADS, A_GROUP, HEAD_DIM), q_gain).transpose(0, 2, 3, 1, 4)
    k = rms_norm(k.reshape(bsz, n, A_KV_HEADS, HEAD_DIM), k_gain).transpose(0, 2, 1, 3)
    v = v.reshape(bsz, n, A_KV_HEADS, HEAD_DIM).transpose(0, 2, 1, 3)
    k_all = jnp.concatenate([k[..., :n_ctx, :], apply_axial_rope(k[..., n_ctx:, :], cos, sin)], axis=-2)
    q_lat = apply_axial_rope(q[..., n_ctx:, :], cos, sin)
    o_lat = map_query_blocks(lambda qb: gqa_attend(qb, k_all, v), q_lat)

    def finish(o, uu):
        o = o.transpose(0, 3, 1, 2, 4).reshape(o.shape[0], o.shape[3], A_WIDTH)
        return jnp.concatenate([o, pool_mixer(uu, w_pool, pool_scale)], axis=-1) @ w_out

    y = finish(o_lat, u[:, n_ctx:])
    yc = None
    if with_ctx_out:
        o_ctx = gqa_attend(q[..., :n_ctx, :], k[..., :n_ctx, :], v[:, :, :n_ctx])
        yc = finish(o_ctx, u[:, :n_ctx])
    return y, yc


def mixer_cd(h, hc, w_in, lq1, lk1, lq2, lk2, subln_gain, conv_w, conv_b, conv_ln_g, conv_ln_b,
             w_out, cos, sin, lam_init, with_ctx_out):
    bsz, n_lat, _ = h.shape
    n_ctx = hc.shape[1]
    n = n_ctx + n_lat
    p = jnp.concatenate([hc, h], axis=1) @ w_in
    q, k, v, u = jnp.split(p, [C_WIDTH, 2 * C_WIDTH, 3 * C_WIDTH], axis=-1)
    q = q.reshape(bsz, n, C_HEADS, 2, C_QK_DIM).transpose(0, 2, 3, 1, 4)
    k = k.reshape(bsz, n, C_HEADS, 2, C_QK_DIM).transpose(0, 2, 3, 1, 4)
    v = v.reshape(bsz, n, C_HEADS, C_V_DIM).transpose(0, 2, 1, 3)
    lam = (jnp.exp(jnp.sum(lq1.astype(F32) * lk1.astype(F32)))
           - jnp.exp(jnp.sum(lq2.astype(F32) * lk2.astype(F32))) + lam_init)
    k_all = jnp.concatenate([k[..., :n_ctx, :], apply_axial_rope(k[..., n_ctx:, :], cos, sin)], axis=-2)
    q_lat = apply_axial_rope(q[..., n_ctx:, :], cos, sin)
    o_lat = map_query_blocks(lambda qb: diff_attend(qb, k_all, v, lam), q_lat)

    def finish(o, uu):
        o = rms_norm(o, subln_gain) * (1.0 - lam_init)
        o = o.transpose(0, 2, 1, 3).reshape(o.shape[0], o.shape[2], C_WIDTH)
        conv = conformer_conv(uu, conv_w, conv_b, conv_ln_g, conv_ln_b)
        return jnp.concatenate([o, conv], axis=-1) @ w_out

    y = finish(o_lat, u[:, n_ctx:])
    yc = None
    if with_ctx_out:
        o_ctx = diff_attend(q[..., :n_ctx, :], k[..., :n_ctx, :], v[:, :, :n_ctx], lam)
        yc = finish(o_ctx, u[:, :n_ctx])
    return y, yc


def setup_inputs(seed: int = 0) -> dict:
    key = jax.random.key(seed)
    ks = iter(jax.random.split(key, 32))
    n_even = (DEPTH + 1) // 2
    n_odd = DEPTH // 2

    def nrm(shape, scale):
        return jax.random.normal(next(ks), shape, F32) * scale

    def gain(shape):
        return 1.0 + nrm(shape, 0.05)

    return {
        'x': nrm((BATCH, SEQ, D_MODEL), 1.0),
        'c': nrm((BATCH, D_MODEL), 1.0),
        'ctx': nrm((BATCH, CTX_LEN, D_MODEL), 1.0),
        'c_ctx': nrm((D_MODEL,), 1.0),
        'ab_w_in': nrm((n_even, D_MODEL, A_IN_WIDTH), D_MODEL ** -0.5),
        'ab_q_gain': gain((n_even, HEAD_DIM)),
        'ab_k_gain': gain((n_even, HEAD_DIM)),
        'ab_w_pool': nrm((n_even, B_GROUPS, B_GROUP_DIM, B_GROUP_DIM), B_GROUP_DIM ** -0.5),
        'ab_pool_scale': gain((n_even, B_WIDTH)),
        'ab_w_out': nrm((n_even, D_MODEL, D_MODEL), BETA * D_MODEL ** -0.5),
        'cd_w_in': nrm((n_odd, D_MODEL, C_IN_WIDTH), D_MODEL ** -0.5),
        'cd_lambda_q1': nrm((n_odd, C_QK_DIM), 0.1),
        'cd_lambda_k1': nrm((n_odd, C_QK_DIM), 0.1),
        'cd_lambda_q2': nrm((n_odd, C_QK_DIM), 0.1),
        'cd_lambda_k2': nrm((n_odd, C_QK_DIM), 0.1),
        'cd_subln_gain': gain((n_odd, C_V_DIM)),
        'cd_conv_w': nrm((n_odd, CONV_WIDTH, D_CH), CONV_WIDTH ** -0.5),
        'cd_conv_b': nrm((n_odd, D_CH), 0.02),
        'cd_conv_ln_g': gain((n_odd, D_CH)),
        'cd_conv_ln_b': nrm((n_odd, D_CH), 0.02),
        'cd_w_out': nrm((n_odd, D_MODEL, D_MODEL), BETA * D_MODEL ** -0.5),
        'ada_w': nrm((DEPTH, D_MODEL, 6 * D_MODEL), 0.5 * D_MODEL ** -0.5),
        'ada_b': nrm((DEPTH, 6 * D_MODEL), 0.02),
        'ln1_g': gain((DEPTH, D_MODEL)),
        'ln1_b': nrm((DEPTH, D_MODEL), 0.02),
        'ln2_g': gain((DEPTH, D_MODEL)),
        'ln2_b': nrm((DEPTH, D_MODEL), 0.02),
        'ffn_w_in': nrm((DEPTH, D_MODEL, 2 * FFN_HIDDEN), D_MODEL ** -0.5),
        'ffn_w_out': nrm((DEPTH, FFN_HIDDEN, D_MODEL), BETA * FFN_HIDDEN ** -0.5),
    }


def reference(x, c, ctx, c_ctx, ab_w_in, ab_q_gain, ab_k_gain, ab_w_pool, ab_pool_scale, ab_w_out,
              cd_w_in, cd_lambda_q1, cd_lambda_k1, cd_lambda_q2, cd_lambda_k2, cd_subln_gain,
              cd_conv_w, cd_conv_b, cd_conv_ln_g, cd_conv_ln_b, cd_w_out,
              ada_w, ada_b, ln1_g, ln1_b, ln2_g, ln2_b, ffn_w_in, ffn_w_out):
    cos, sin = axial_rope_tables(x.shape[1])
    xc = ctx
    for l in range(DEPTH):
        with_ctx_out = l < DEPTH - 1
        mod = (jax.nn.silu(c) @ ada_w[l] + ada_b[l])[:, None, :]
        mod_c = (jax.nn.silu(c_ctx) @ ada_w[l] + ada_b[l])[None, None, :]
        sh1, sc1, g1, sh2, sc2, g2 = jnp.split(mod, 6, axis=-1)
        csh1, csc1, cg1, csh2, csc2, cg2 = jnp.split(mod_c, 6, axis=-1)
        h = x * (1.0 + sc1) + sh1
        hc = xc * (1.0 + csc1) + csh1
        i = l // 2
        if l % 2 == 0:
            y, yc = mixer_ab(h, hc, ab_w_in[i], ab_q_gain[i], ab_k_gain[i], ab_w_pool[i], ab_pool_scale[i],
                             ab_w_out[i], cos, sin, with_ctx_out)
        else:
            lam_init = 0.8 - 0.6 * math.exp(-0.3 * l)
            y, yc = mixer_cd(h, hc, cd_w_in[i], cd_lambda_q1[i], cd_lambda_k1[i], cd_lambda_q2[i], cd_lambda_k2[i],
                             cd_subln_gain[i], cd_conv_w[i], cd_conv_b[i], cd_conv_ln_g[i], cd_conv_ln_b[i],
                             cd_w_out[i], cos, sin, lam_init, with_ctx_out)
        x = layer_norm(ALPHA * x + g1 * y, ln1_g[l], ln1_b[l])
        x = layer_norm(ALPHA * x + g2 * swiglu(x * (1.0 + sc2) + sh2, ffn_w_in[l], ffn_w_out[l]), ln2_g[l], ln2_b[l])
        if with_ctx_out:
            xc = layer_norm(ALPHA * xc + cg1 * yc, ln1_g[l], ln1_b[l])
            xc = layer_norm(ALPHA * xc + cg2 * swiglu(xc * (1.0 + csc2) + csh2, ffn_w_in[l], ffn_w_out[l]),
                            ln2_g[l], ln2_b[l])
    return x
```

```python
import functools
import math
from typing import NamedTuple

import jax
import jax.numpy as jnp
from jax import lax
from jax.experimental import pallas as pl
from jax.experimental.pallas import tpu as pltpu

F32 = jnp.float32
BF16 = jnp.bfloat16

D_MODEL = 1024
DEPTH = 2
GRID_W = 64
HEAD_DIM = 64
ROPE_THETA = 10000.0
EPS = 1e-6

B_WIDTH = D_MODEL // 4
B_GROUPS = 4
B_GROUP_DIM = B_WIDTH // B_GROUPS
POOL_WINDOWS = (2, 4, 8, 16)
A_WIDTH = D_MODEL - B_WIDTH
A_KV_WIDTH = 4 * HEAD_DIM
D_CH = D_MODEL // 4
C_WIDTH = D_MODEL - D_CH
C_V_DIM = 2 * HEAD_DIM
CONV_WIDTH = 31
FFN_HIDDEN = -(-8 * D_MODEL // (3 * 256)) * 256
ALPHA = (2.0 * DEPTH) ** 0.25

LANES = 128
MXU_DIM = 256
VMEM_LIMIT = 48 * 1024 * 1024

POOL_PAD = 8
CONV_PAD = 16
FFN_CHUNK = 256


def _params(*semantics):
    return pltpu.CompilerParams(dimension_semantics=semantics, vmem_limit_bytes=VMEM_LIMIT)


def _layer_norm(z, g, b):
    mu = jnp.mean(z, axis=-1, keepdims=True)
    zc = z - mu
    var = jnp.mean(zc * zc, axis=-1, keepdims=True)
    return zc * lax.rsqrt(var + EPS) * g + b


def _dot(a, b):
    return jnp.dot(a, b, preferred_element_type=F32)


def _dot_nt(a, b):
    return lax.dot_general(a, b, (((1,), (1,)), ((), ())), preferred_element_type=F32)


def _mod_kernel(c_ref, w_ref, b_ref, o_ref):
    s = jax.nn.silu(c_ref[...])
    o_ref[...] = jnp.dot(s, w_ref[...], preferred_element_type=F32,
                         precision=lax.Precision.HIGHEST) + b_ref[...]


def _modulation(cc, ada_w, ada_b):
    rows = cc.shape[0]
    tn = 1536
    out = pl.pallas_call(
        _mod_kernel,
        grid=(DEPTH, 6 * D_MODEL // tn),
        in_specs=[
            pl.BlockSpec((rows, D_MODEL), lambda l, j: (0, 0)),
            pl.BlockSpec((None, D_MODEL, tn), lambda l, j: (l, 0, j)),
            pl.BlockSpec((None, 1, tn), lambda l, j: (l, 0, j)),
        ],
        out_specs=pl.BlockSpec((None, rows, tn), lambda l, j: (l, 0, j)),
        out_shape=jax.ShapeDtypeStruct((DEPTH, rows, 6 * D_MODEL), F32),
        compiler_params=_params("parallel", "parallel"),
        name="adaln_mod",
    )(cc, ada_w, ada_b.reshape(DEPTH, 1, 6 * D_MODEL))
    return out.reshape(DEPTH, rows, 1, 6 * D_MODEL)


def _mod_spec(chunk, per_batch, ctx_row):
    if per_batch:
        return pl.BlockSpec((None, 1, D_MODEL), lambda b, *_: (b, 0, chunk))
    return pl.BlockSpec((None, 1, D_MODEL), lambda b, *_: (ctx_row, 0, chunk))


class Split(NamedTuple):
    start: int
    width: int
    dtype: object
    scale: float
    norm: bool
    rope: bool


def _group_rms_norm(p, gain, gmat):
    ss = p * p
    hi = ss.astype(BF16)
    lo = (ss - hi.astype(F32)).astype(BF16)
    blocks = []
    for j in range(p.shape[1] // MXU_DIM):
        sl = slice(j * MXU_DIM, (j + 1) * MXU_DIM)
        blocks.append(_dot(hi[:, sl], gmat) + _dot(lo[:, sl], gmat))
    ms = blocks[0] if len(blocks) == 1 else jnp.concatenate(blocks, axis=1)
    return p * lax.rsqrt(ms + EPS) * gain


def _rope(p, cos, sin):
    tm = p.shape[0]
    lane = lax.broadcasted_iota(jnp.int32, (tm, LANES), 1)
    first = (lane & 16) == 0
    outs = []
    for j in range(p.shape[1] // LANES):
        xb = p[:, j * LANES:(j + 1) * LANES]
        partner = jnp.where(first, pltpu.roll(xb, LANES - 16, 1), pltpu.roll(xb, 16, 1))
        outs.append(xb * cos + partner * sin)
    return jnp.concatenate(outs, axis=1)


def _inproj_kernel(x_ref, sc_ref, sh_ref, w_ref, *rest, splits, use_norm, use_rope):
    idx = 0
    if use_norm:
        gain_ref, gmat_ref = rest[0], rest[1]
        idx = 2
    if use_rope:
        cos_ref, sin_ref = rest[idx], rest[idx + 1]
        idx += 2
    out_refs = rest[idx:]
    h = (x_ref[...] * (1.0 + sc_ref[...]) + sh_ref[...]).astype(BF16)
    gain_off = 0
    for sp, o_ref in zip(splits, out_refs):
        p = _dot(h, w_ref[:, sp.start:sp.start + sp.width])
        if sp.norm:
            p = _group_rms_norm(p, gain_ref[:, gain_off:gain_off + sp.width], gmat_ref[...])
            gain_off += sp.width
        if sp.rope and use_rope:
            p = _rope(p, cos_ref[...], sin_ref[...])
        if sp.scale != 1.0:
            p = p * sp.scale
        o_ref[...] = p.astype(o_ref.dtype)


def _inproj(x, mod, w, splits, *, tm, per_batch, ctx_row, gain=None, gmat=None, rope=None):
    bsz, rows, _ = x.shape
    n_cols = w.shape[1]
    use_norm = gain is not None
    use_rope = rope is not None
    in_specs = [
        pl.BlockSpec((None, tm, D_MODEL), lambda b, i: (b, i, 0)),
        _mod_spec(1, per_batch, ctx_row),
        _mod_spec(0, per_batch, ctx_row),
        pl.BlockSpec((D_MODEL, n_cols), lambda b, i: (0, 0)),
    ]
    args = [x, mod, mod, w]
    if use_norm:
        in_specs += [pl.BlockSpec(gain.shape, lambda b, i: (0, 0)),
                     pl.BlockSpec(gmat.shape, lambda b, i: (0, 0))]
        args += [gain, gmat]
    if use_rope:
        in_specs += [pl.BlockSpec((tm, LANES), lambda b, i: (i, 0))] * 2
        args += [rope[0], rope[1]]
    out_specs = [pl.BlockSpec((None, tm, sp.width), lambda b, i: (b, i, 0)) for sp in splits]
    out_shape = [jax.ShapeDtypeStruct((bsz, rows, sp.width), sp.dtype) for sp in splits]
    return pl.pallas_call(
        functools.partial(_inproj_kernel, splits=tuple(splits), use_norm=use_norm, use_rope=use_rope),
        grid=(bsz, rows // tm),
        in_specs=in_specs,
        out_specs=out_specs,
        out_shape=out_shape,
        compiler_params=_params("parallel", "parallel"),
        name="inproj",
    )(*args)


def _half_masks(shape, dtype):
    lane = lax.broadcasted_iota(jnp.int32, shape, 1)
    lo = (lane < HEAD_DIM).astype(dtype)
    return lo, (1 - lo).astype(dtype)


def _softmax_parts(q, k_refs):
    ss = [_dot_nt(q, k_ref[...]) for k_ref in k_refs]
    m = functools.reduce(jnp.maximum, [jnp.max(s, axis=1, keepdims=True) for s in ss])
    es = [jnp.exp(s - m) for s in ss]
    l = functools.reduce(jnp.add, [jnp.sum(e, axis=1, keepdims=True) for e in es])
    return es, l


def _attn_gqa_kernel(q_ref, *refs, n_parts):
    k_refs = refs[:n_parts]
    v_refs = refs[n_parts:2 * n_parts]
    o_ref = refs[2 * n_parts]
    tq = q_ref.shape[0]
    m_lo, m_hi = _half_masks((tq, LANES), BF16)
    lane = lax.broadcasted_iota(jnp.int32, (tq, LANES), 1)
    for pr in range(q_ref.shape[1] // LANES):
        qp = q_ref[:, pr * LANES:(pr + 1) * LANES]
        halves = []
        for mask in (m_lo, m_hi):
            es, l = _softmax_parts(qp * mask, k_refs)
            o = functools.reduce(jnp.add, [_dot(e.astype(BF16), v_ref[...]) for e, v_ref in zip(es, v_refs)])
            halves.append(o / l)
        o_ref[:, pr * LANES:(pr + 1) * LANES] = jnp.where(lane < HEAD_DIM, halves[0], halves[1]).astype(o_ref.dtype)


def _attn_gqa(q, ks, vs, *, tq):
    bsz, rows, _ = q.shape
    n_parts = len(ks)
    qw = A_WIDTH // 2
    kv_specs = [pl.BlockSpec((None, k.shape[1], LANES), lambda b, p, i: (b, 0, p)) for k in ks]
    return pl.pallas_call(
        functools.partial(_attn_gqa_kernel, n_parts=n_parts),
        grid=(bsz, 2, rows // tq),
        in_specs=[pl.BlockSpec((None, tq, qw), lambda b, p, i: (b, i, p))] + kv_specs + kv_specs,
        out_specs=pl.BlockSpec((None, tq, qw), lambda b, p, i: (b, i, p)),
        out_shape=jax.ShapeDtypeStruct((bsz, rows, A_WIDTH), BF16),
        compiler_params=_params("parallel", "parallel", "parallel"),
        name="attn_gqa",
    )(q, *ks, *vs)


def _attn_diff_kernel(lam_ref, gain_ref, q_ref, *refs, n_parts, lam_init):
    k_refs = refs[:n_parts]
    v_refs = refs[n_parts:2 * n_parts]
    o_ref = refs[2 * n_parts]
    tq = q_ref.shape[0]
    lv = lam_ref[...]
    lam = (jnp.exp(jnp.sum(lv[0:1] * lv[1:2], axis=1, keepdims=True))
           - jnp.exp(jnp.sum(lv[2:3] * lv[3:4], axis=1, keepdims=True)) + lam_init)
    m_lo, m_hi = _half_masks((tq, LANES), BF16)
    q = q_ref[...]
    es0, l0 = _softmax_parts(q * m_lo, k_refs)
    es1, l1 = _softmax_parts(q * m_hi, k_refs)
    r0 = 1.0 / l0
    r1 = lam / l1
    o = functools.reduce(jnp.add, [_dot((e0 * r0 - e1 * r1).astype(BF16), v_ref[...])
                                   for e0, e1, v_ref in zip(es0, es1, v_refs)])
    ms = jnp.mean(o * o, axis=1, keepdims=True)
    o_ref[...] = (o * lax.rsqrt(ms + EPS) * gain_ref[...] * (1.0 - lam_init)).astype(o_ref.dtype)


def _attn_diff(lam_vecs, gain, q, ks, vs, *, tq, lam_init):
    bsz, rows, _ = q.shape
    n_parts = len(ks)
    heads = C_WIDTH // C_V_DIM
    kv_specs = [pl.BlockSpec((None, k.shape[1], LANES), lambda b, h, i: (b, 0, h)) for k in ks]
    return pl.pallas_call(
        functools.partial(_attn_diff_kernel, n_parts=n_parts, lam_init=lam_init),
        grid=(bsz, heads, rows // tq),
        in_specs=[pl.BlockSpec(lam_vecs.shape, lambda b, h, i: (0, 0)),
                  pl.BlockSpec(gain.shape, lambda b, h, i: (0, 0)),
                  pl.BlockSpec((None, tq, LANES), lambda b, h, i: (b, i, h))] + kv_specs + kv_specs,
        out_specs=pl.BlockSpec((None, tq, LANES), lambda b, h, i: (b, i, h)),
        out_shape=jax.ShapeDtypeStruct((bsz, rows, C_WIDTH), BF16),
        compiler_params=_params("parallel", "parallel", "parallel"),
        name="attn_diff",
    )(lam_vecs, gain, q, *ks, *vs)


def _pool_kernel(u_ref, w_ref, ps_ref, o_ref, pad_ref, *, chunk):
    t_len = u_ref.shape[0]
    zeros = jnp.zeros((POOL_PAD, B_WIDTH), F32)
    pad_ref[0:POOL_PAD, :] = zeros
    pad_ref[POOL_PAD + t_len:2 * POOL_PAD + t_len, :] = zeros
    pad_ref[POOL_PAD:POOL_PAD + t_len, :] = u_ref[...]
    grp = lax.broadcasted_iota(jnp.int32, (chunk, B_WIDTH), 1) // B_GROUP_DIM
    for c in range(t_len // chunk):
        r0 = c * chunk

        def win(j):
            return pad_ref[POOL_PAD + r0 + j:POOL_PAD + r0 + j + chunk, :]

        tok = win(0)
        sums = []
        acc = None
        for w in POOL_WINDOWS:
            lo, hi = -(w // 2), w - 1 - w // 2
            if acc is None:
                acc = functools.reduce(jnp.add, [win(j) for j in range(lo, hi + 1)])
            else:
                plo, phi = -(prev_w // 2), prev_w - 1 - prev_w // 2
                acc = functools.reduce(jnp.add, [acc] + [win(j) for j in range(lo, plo)]
                                       + [win(j) for j in range(phi + 1, hi + 1)])
            prev_w = w
            sums.append(acc)
        t = lax.broadcasted_iota(jnp.int32, (chunk, B_WIDTH), 0) + r0
        half = jnp.where(grp == 0, 1, jnp.where(grp == 1, 2, jnp.where(grp == 2, 4, 8)))
        tail = half - 1
        cnt = (jnp.minimum(t + tail + 1, t_len) - jnp.maximum(t - half, 0)).astype(F32)
        s = jnp.where(grp == 0, sums[0], jnp.where(grp == 1, sums[1], jnp.where(grp == 2, sums[2], sums[3])))
        pooled = s / cnt - tok
        mixed = _dot(pooled.astype(BF16), w_ref[...]) * ps_ref[...]
        o_ref[r0:r0 + chunk, :] = mixed.astype(o_ref.dtype)


def _pool(u, w_bd, pool_scale):
    bsz, t_len, _ = u.shape
    return pl.pallas_call(
        functools.partial(_pool_kernel, chunk=min(t_len, 256)),
        grid=(bsz,),
        in_specs=[pl.BlockSpec((None, t_len, B_WIDTH), lambda b: (b, 0, 0)),
                  pl.BlockSpec(w_bd.shape, lambda b: (0, 0)),
                  pl.BlockSpec(pool_scale.shape, lambda b: (0, 0))],
        out_specs=pl.BlockSpec((None, t_len, B_WIDTH), lambda b: (b, 0, 0)),
        out_shape=jax.ShapeDtypeStruct((bsz, t_len, B_WIDTH), BF16),
        scratch_shapes=[pltpu.VMEM((t_len + 2 * POOL_PAD, B_WIDTH), F32)],
        compiler_params=_params("parallel"),
        name="pool_mixer",
    )(u, w_bd, pool_scale)


def _conv_kernel(u_ref, w_ref, b_ref, g_ref, beta_ref, o_ref, pad_ref, *, chunk):
    t_len = u_ref.shape[0]
    zeros = jnp.zeros((CONV_PAD, D_CH), F32)
    pad_ref[0:CONV_PAD, :] = zeros
    pad_ref[CONV_PAD + t_len:2 * CONV_PAD + t_len, :] = zeros
    pad_ref[CONV_PAD:CONV_PAD + t_len, :] = u_ref[:, 0:D_CH] * jax.nn.sigmoid(u_ref[:, D_CH:2 * D_CH])
    half = CONV_WIDTH // 2
    for c in range(t_len // chunk):
        r0 = CONV_PAD + c * chunk - half
        acc = pad_ref[r0:r0 + chunk, :] * w_ref[0:1, :]
        for j in range(1, CONV_WIDTH):
            acc = acc + pad_ref[r0 + j:r0 + j + chunk, :] * w_ref[j:j + 1, :]
        z = _layer_norm(acc + b_ref[...], g_ref[...], beta_ref[...])
        o_ref[c * chunk:(c + 1) * chunk, :] = jax.nn.silu(z).astype(o_ref.dtype)


def _conformer_conv(u, conv_w, conv_b, ln_g, ln_b):
    bsz, t_len, _ = u.shape
    vec = pl.BlockSpec((1, D_CH), lambda b: (0, 0))
    return pl.pallas_call(
        functools.partial(_conv_kernel, chunk=128),
        grid=(bsz,),
        in_specs=[pl.BlockSpec((None, t_len, 2 * D_CH), lambda b: (b, 0, 0)),
                  pl.BlockSpec(conv_w.shape, lambda b: (0, 0)), vec, vec, vec],
        out_specs=pl.BlockSpec((None, t_len, D_CH), lambda b: (b, 0, 0)),
        out_shape=jax.ShapeDtypeStruct((bsz, t_len, D_CH), BF16),
        scratch_shapes=[pltpu.VMEM((t_len + 2 * CONV_PAD, D_CH), F32)],
        compiler_params=_params("parallel"),
        name="conformer_conv",
    )(u, conv_w, conv_b, ln_g, ln_b)


def _outproj_kernel(a_ref, m_ref, w_ref, x_ref, gate_ref, g_ref, b_ref, o_ref):
    wa = a_ref.shape[1]
    y = _dot(a_ref[...], w_ref[0:wa, :]) + _dot(m_ref[...], w_ref[wa:, :])
    z = ALPHA * x_ref[...] + gate_ref[...] * y
    o_ref[...] = _layer_norm(z, g_ref[...], b_ref[...])


def _outproj(attn, mix, w_out, x, mod, ln_g, ln_b, *, tm, per_batch, ctx_row):
    bsz, rows, _ = x.shape
    vec = pl.BlockSpec((1, D_MODEL), lambda b, i: (0, 0))
    return pl.pallas_call(
        _outproj_kernel,
        grid=(bsz, rows // tm),
        in_specs=[pl.BlockSpec((None, tm, attn.shape[2]), lambda b, i: (b, i, 0)),
                  pl.BlockSpec((None, tm, mix.shape[2]), lambda b, i: (b, i, 0)),
                  pl.BlockSpec(w_out.shape, lambda b, i: (0, 0)),
                  pl.BlockSpec((None, tm, D_MODEL), lambda b, i: (b, i, 0)),
                  _mod_spec(2, per_batch, ctx_row), vec, vec],
        out_specs=pl.BlockSpec((None, tm, D_MODEL), lambda b, i: (b, i, 0)),
        out_shape=jax.ShapeDtypeStruct(x.shape, F32),
        compiler_params=_params("parallel", "parallel"),
        name="outproj_ln",
    )(attn, mix, w_out, x, mod, ln_g, ln_b)


def _ffn_kernel(x_ref, sc_ref, sh_ref, gate_ref, wa_ref, wg_ref, wo_ref, g_ref, b_ref, o_ref, h_ref, acc_ref):
    k = pl.program_id(2)

    @pl.when(k == 0)
    def _():
        h_ref[...] = (x_ref[...] * (1.0 + sc_ref[...]) + sh_ref[...]).astype(BF16)

    h = h_ref[...]
    a = _dot(h, wa_ref[...])
    g = _dot(h, wg_ref[...])
    y = _dot((jax.nn.silu(g) * a).astype(BF16), wo_ref[...])

    @pl.when(k == 0)
    def _():
        acc_ref[...] = y

    @pl.when(k > 0)
    def _():
        acc_ref[...] += y

    @pl.when(k == pl.num_programs(2) - 1)
    def _():
        z = ALPHA * x_ref[...] + gate_ref[...] * acc_ref[...]
        o_ref[...] = _layer_norm(z, g_ref[...], b_ref[...])


def _ffn(x, mod, w_in, w_out, ln_g, ln_b, *, tm, per_batch, ctx_row):
    bsz, rows, _ = x.shape
    n_chunks = FFN_HIDDEN // FFN_CHUNK
    vec = pl.BlockSpec((1, D_MODEL), lambda b, i, k: (0, 0))
    return pl.pallas_call(
        _ffn_kernel,
        grid=(bsz, rows // tm, n_chunks),
        in_specs=[pl.BlockSpec((None, tm, D_MODEL), lambda b, i, k: (b, i, 0)),
                  _mod_spec(4, per_batch, ctx_row),
                  _mod_spec(3, per_batch, ctx_row),
                  _mod_spec(5, per_batch, ctx_row),
                  pl.BlockSpec((D_MODEL, FFN_CHUNK), lambda b, i, k: (0, k)),
                  pl.BlockSpec((D_MODEL, FFN_CHUNK), lambda b, i, k: (0, k + n_chunks)),
                  pl.BlockSpec((FFN_CHUNK, D_MODEL), lambda b, i, k: (k, 0)),
                  vec, vec],
        out_specs=pl.BlockSpec((None, tm, D_MODEL), lambda b, i, k: (b, i, 0)),
        out_shape=jax.ShapeDtypeStruct(x.shape, F32),
        scratch_shapes=[pltpu.VMEM((tm, D_MODEL), BF16), pltpu.VMEM((tm, D_MODEL), F32)],
        compiler_params=_params("parallel", "parallel", "arbitrary"),
        name="ffn_ln",
    )(x, mod, mod, mod, w_in, w_in, w_out, ln_g, ln_b)


def _rope_tables(n_tokens):
    rows = n_tokens // GRID_W
    row = jnp.repeat(jnp.arange(rows, dtype=F32), GRID_W)
    col = jnp.tile(jnp.arange(GRID_W, dtype=F32), rows)
    axis_dim = HEAD_DIM // 2
    freqs = ROPE_THETA ** (-jnp.arange(0, axis_dim, 2, dtype=F32) / axis_dim)
    ang_r, ang_c = row[:, None] * freqs, col[:, None] * freqs
    cos = jnp.concatenate([jnp.cos(ang_r)] * 2 + [jnp.cos(ang_c)] * 2, axis=-1)
    sin = jnp.concatenate([-jnp.sin(ang_r), jnp.sin(ang_r), -jnp.sin(ang_c), jnp.sin(ang_c)], axis=-1)
    return jnp.tile(cos, (1, 2)), jnp.tile(sin, (1, 2))


def _pair_heads_cols(w):
    return w.reshape(w.shape[0], 2, 2, 3, HEAD_DIM).transpose(0, 1, 3, 2, 4).reshape(w.shape[0], A_WIDTH)


def _pair_heads_rows(w):
    return w.reshape(2, 2, 3, HEAD_DIM, w.shape[1]).transpose(0, 2, 1, 3, 4).reshape(A_WIDTH, w.shape[1])


def _block_diag(w_pool):
    out = jnp.zeros((B_WIDTH, B_WIDTH), w_pool.dtype)
    for g in range(B_GROUPS):
        sl = slice(g * B_GROUP_DIM, (g + 1) * B_GROUP_DIM)
        out = out.at[sl, sl].set(w_pool[g])
    return out


def kernel(x, c, ctx, c_ctx, ab_w_in, ab_q_gain, ab_k_gain, ab_w_pool, ab_pool_scale, ab_w_out,
           cd_w_in, cd_lambda_q1, cd_lambda_k1, cd_lambda_q2, cd_lambda_k2, cd_subln_gain,
           cd_conv_w, cd_conv_b, cd_conv_ln_g, cd_conv_ln_b, cd_w_out,
           ada_w, ada_b, ln1_g, ln1_b, ln2_g, ln2_b, ffn_w_in, ffn_w_out):
    bsz, t_len, _ = x.shape
    n_ctx = ctx.shape[1]
    ctx_row = bsz
    mod_rows = -(-(bsz + 1) // 8) * 8
    cc = jnp.zeros((mod_rows, D_MODEL), F32).at[:bsz].set(c).at[bsz].set(c_ctx)
    mods = _modulation(cc, ada_w, ada_b)
    rope = _rope_tables(t_len)
    row = lambda v: v.reshape(1, -1)
    head_of = jnp.arange(MXU_DIM) // HEAD_DIM
    gmat = jnp.where(head_of[:, None] == head_of[None, :], 1.0 / HEAD_DIM, 0.0).astype(BF16)

    l, i = 0, 0
    w_in = jnp.concatenate([_pair_heads_cols(ab_w_in[i][:, :A_WIDTH]), ab_w_in[i][:, A_WIDTH:]], axis=1).astype(BF16)
    w_out = jnp.concatenate([_pair_heads_rows(ab_w_out[i][:A_WIDTH]), ab_w_out[i][A_WIDTH:]], axis=0).astype(BF16)
    gain = jnp.concatenate([jnp.tile(ab_q_gain[i], A_WIDTH // HEAD_DIM),
                            jnp.tile(ab_k_gain[i], A_KV_WIDTH // HEAD_DIM)]).reshape(1, -1)
    splits = [Split(0, A_WIDTH, BF16, HEAD_DIM ** -0.5, True, True),
              Split(A_WIDTH, A_KV_WIDTH, BF16, 1.0, True, True),
              Split(A_WIDTH + A_KV_WIDTH, A_KV_WIDTH, BF16, 1.0, False, False),
              Split(A_WIDTH + 2 * A_KV_WIDTH, B_WIDTH, F32, 1.0, False, False)]
    w_bd = _block_diag(ab_w_pool[i]).astype(BF16)
    pool_scale = row(ab_pool_scale[i])
    w_ffn_in, w_ffn_out = ffn_w_in[l].astype(BF16), ffn_w_out[l].astype(BF16)
    g1, b1, g2, b2 = row(ln1_g[l]), row(ln1_b[l]), row(ln2_g[l]), row(ln2_b[l])

    q, k, v, u = _inproj(x, mods[l], w_in, splits, tm=512, per_batch=True, ctx_row=ctx_row,
                         gain=gain, gmat=gmat, rope=rope)
    qc, kc, vc, uc = _inproj(ctx, mods[l], w_in, splits, tm=n_ctx, per_batch=False, ctx_row=ctx_row,
                             gain=gain, gmat=gmat)
    o = _attn_gqa(q, [kc, k], [vc, v], tq=256)
    oc = _attn_gqa(qc, [kc], [vc], tq=n_ctx)
    mix = _pool(u, w_bd, pool_scale)
    mixc = _pool(uc, w_bd, pool_scale)
    x = _outproj(o, mix, w_out, x, mods[l], g1, b1, tm=512, per_batch=True, ctx_row=ctx_row)
    xc = _outproj(oc, mixc, w_out, ctx, mods[l], g1, b1, tm=n_ctx, per_batch=False, ctx_row=ctx_row)
    x = _ffn(x, mods[l], w_ffn_in, w_ffn_out, g2, b2, tm=512, per_batch=True, ctx_row=ctx_row)
    xc = _ffn(xc.reshape(1, bsz * n_ctx, D_MODEL), mods[l], w_ffn_in, w_ffn_out, g2, b2,
              tm=512, per_batch=False, ctx_row=ctx_row).reshape(bsz, n_ctx, D_MODEL)

    l, i = 1, 0
    lam_init = 0.8 - 0.6 * math.exp(-0.3 * l)
    w_in = cd_w_in[i].astype(BF16)
    w_out = cd_w_out[i].astype(BF16)
    splits = [Split(0, C_WIDTH, BF16, HEAD_DIM ** -0.5, False, True),
              Split(C_WIDTH, C_WIDTH, BF16, 1.0, False, True),
              Split(2 * C_WIDTH, C_WIDTH, BF16, 1.0, False, False),
              Split(3 * C_WIDTH, 2 * D_CH, F32, 1.0, False, False)]
    ctx_splits = [Split(C_WIDTH, C_WIDTH, BF16, 1.0, False, False),
                  Split(2 * C_WIDTH, C_WIDTH, BF16, 1.0, False, False)]
    lam_vecs = jnp.stack([cd_lambda_q1[i], cd_lambda_k1[i], cd_lambda_q2[i], cd_lambda_k2[i]])
    sub_gain = row(cd_subln_gain[i])
    w_ffn_in, w_ffn_out = ffn_w_in[l].astype(BF16), ffn_w_out[l].astype(BF16)
    g1, b1, g2, b2 = row(ln1_g[l]), row(ln1_b[l]), row(ln2_g[l]), row(ln2_b[l])

    q, k, v, u = _inproj(x, mods[l], w_in, splits, tm=512, per_batch=True, ctx_row=ctx_row, rope=rope)
    kc, vc = _inproj(xc, mods[l], w_in, ctx_splits, tm=n_ctx, per_batch=False, ctx_row=ctx_row)
    o = _attn_diff(lam_vecs, sub_gain, q, [kc, k], [vc, v], tq=256, lam_init=lam_init)
    conv = _conformer_conv(u, cd_conv_w[i], row(cd_conv_b[i]), row(cd_conv_ln_g[i]), row(cd_conv_ln_b[i]))
    x = _outproj(o, conv, w_out, x, mods[l], g1, b1, tm=512, per_batch=True, ctx_row=ctx_row)
    x = _ffn(x, mods[l], w_ffn_in, w_ffn_out, g2, b2, tm=512, per_batch=True, ctx_row=ctx_row)
    return x
```

```python
import functools
import math
from typing import NamedTuple

import jax
import jax.numpy as jnp
from jax import lax
from jax.experimental import pallas as pl
from jax.experimental.pallas import tpu as pltpu

F32 = jnp.float32
BF16 = jnp.bfloat16

D_MODEL = 1024
DEPTH = 2
GRID_W = 64
HEAD_DIM = 64
ROPE_THETA = 10000.0
EPS = 1e-6

B_WIDTH = D_MODEL // 4
B_GROUPS = 4
B_GROUP_DIM = B_WIDTH // B_GROUPS
POOL_WINDOWS = (2, 4, 8, 16)
A_WIDTH = D_MODEL - B_WIDTH
A_KV_WIDTH = 4 * HEAD_DIM
D_CH = D_MODEL // 4
C_WIDTH = D_MODEL - D_CH
C_V_DIM = 2 * HEAD_DIM
CONV_WIDTH = 31
FFN_HIDDEN = -(-8 * D_MODEL // (3 * 256)) * 256
ALPHA = (2.0 * DEPTH) ** 0.25

LANES = 128
BF16_SUBLANES = 16
Q_SCALE = HEAD_DIM ** -0.5 * math.log2(math.e)
MXU_DIM = 256
VMEM_LIMIT = 48 * 1024 * 1024

POOL_PAD = 8
CONV_PAD = 16
FFN_CHUNK = 256
KEY_CHUNK = MXU_DIM
QK_LOOKAHEAD = 2
Q_TILE = MXU_DIM
ATTN_TQ = 1024


def _params(*semantics):
    return pltpu.CompilerParams(dimension_semantics=semantics, vmem_limit_bytes=VMEM_LIMIT)


def _layer_norm(z, g, b):
    mu = jnp.mean(z, axis=-1, keepdims=True)
    zc = z - mu
    var = jnp.mean(zc * zc, axis=-1, keepdims=True)
    return zc * lax.rsqrt(var + EPS) * g + b


def _dot(a, b):
    return jnp.dot(a, b, preferred_element_type=F32)


def _dot_nt(a, b):
    return lax.dot_general(a, b, (((1,), (1,)), ((), ())), preferred_element_type=F32)


def _mod_kernel(c_ref, w_ref, b_ref, o_ref):
    s = jax.nn.silu(c_ref[...])
    o_ref[...] = jnp.dot(s, w_ref[...], preferred_element_type=F32,
                         precision=lax.Precision.HIGHEST) + b_ref[...]


def _modulation(cc, ada_w, ada_b):
    rows = cc.shape[0]
    tn = 1536
    out = pl.pallas_call(
        _mod_kernel,
        grid=(DEPTH, 6 * D_MODEL // tn),
        in_specs=[
            pl.BlockSpec((rows, D_MODEL), lambda l, j: (0, 0)),
            pl.BlockSpec((None, D_MODEL, tn), lambda l, j: (l, 0, j)),
            pl.BlockSpec((None, 1, tn), lambda l, j: (l, 0, j)),
        ],
        out_specs=pl.BlockSpec((None, rows, tn), lambda l, j: (l, 0, j)),
        out_shape=jax.ShapeDtypeStruct((DEPTH, rows, 6 * D_MODEL), F32),
        compiler_params=_params("parallel", "parallel"),
        name="adaln_mod",
    )(cc, ada_w, ada_b.reshape(DEPTH, 1, 6 * D_MODEL))
    return out.reshape(DEPTH, rows, 1, 6 * D_MODEL)


def _mod_spec(chunk, per_batch, ctx_row):
    if per_batch:
        return pl.BlockSpec((None, 1, D_MODEL), lambda b, *_: (b, 0, chunk))
    return pl.BlockSpec((None, 1, D_MODEL), lambda b, *_: (ctx_row, 0, chunk))


class Split(NamedTuple):
    start: int
    width: int
    dtype: object
    scale: float
    norm: bool
    rope: bool


def _group_rms_norm(p, gain, gmat):
    ss = p * p
    hi = ss.astype(BF16)
    lo = (ss - hi.astype(F32)).astype(BF16)
    blocks = []
    for j in range(p.shape[1] // MXU_DIM):
        sl = slice(j * MXU_DIM, (j + 1) * MXU_DIM)
        blocks.append(_dot(hi[:, sl], gmat) + _dot(lo[:, sl], gmat))
    ms = blocks[0] if len(blocks) == 1 else jnp.concatenate(blocks, axis=1)
    return p * lax.rsqrt(ms + EPS) * gain


def _rope(p, cos, sin):
    tm = p.shape[0]
    lane = lax.broadcasted_iota(jnp.int32, (tm, LANES), 1)
    first = (lane & 16) == 0
    outs = []
    for j in range(p.shape[1] // LANES):
        xb = p[:, j * LANES:(j + 1) * LANES]
        partner = jnp.where(first, pltpu.roll(xb, LANES - 16, 1), pltpu.roll(xb, 16, 1))
        outs.append(xb * cos + partner * sin)
    return jnp.concatenate(outs, axis=1)


def _inproj_kernel(x_ref, sc_ref, sh_ref, w_ref, wvt_ref, *rest, splits, use_norm, use_rope):
    idx = 0
    if use_norm:
        gain_ref, gmat_ref = rest[0], rest[1]
        idx = 2
    if use_rope:
        cos_ref, sin_ref = rest[idx], rest[idx + 1]
        idx += 2
    out_refs = rest[idx:-1]
    vt_ref = rest[-1]
    h = (x_ref[...] * (1.0 + sc_ref[...]) + sh_ref[...]).astype(BF16)
    vt_ref[...] = _dot_nt(wvt_ref[...], h).astype(vt_ref.dtype)
    gain_off = 0
    for sp, o_ref in zip(splits, out_refs):
        p = _dot(h, w_ref[:, sp.start:sp.start + sp.width])
        if sp.norm:
            p = _group_rms_norm(p, gain_ref[:, gain_off:gain_off + sp.width], gmat_ref[...])
            gain_off += sp.width
        if sp.rope and use_rope:
            p = _rope(p, cos_ref[...], sin_ref[...])
        if sp.scale != 1.0:
            p = p * sp.scale
        o_ref[...] = p.astype(o_ref.dtype)


def _inproj(x, mod, w, wvt, splits, *, tm, per_batch, ctx_row, gain=None, gmat=None, rope=None):
    bsz, rows, _ = x.shape
    n_cols = w.shape[1]
    v_width = wvt.shape[0]
    use_norm = gain is not None
    use_rope = rope is not None
    in_specs = [
        pl.BlockSpec((None, tm, D_MODEL), lambda b, i: (b, i, 0)),
        _mod_spec(1, per_batch, ctx_row),
        _mod_spec(0, per_batch, ctx_row),
        pl.BlockSpec((D_MODEL, n_cols), lambda b, i: (0, 0)),
        pl.BlockSpec((v_width, D_MODEL), lambda b, i: (0, 0)),
    ]
    args = [x, mod, mod, w, wvt]
    if use_norm:
        in_specs += [pl.BlockSpec(gain.shape, lambda b, i: (0, 0)),
                     pl.BlockSpec(gmat.shape, lambda b, i: (0, 0))]
        args += [gain, gmat]
    if use_rope:
        in_specs += [pl.BlockSpec((tm, LANES), lambda b, i: (i, 0))] * 2
        args += [rope[0], rope[1]]
    out_specs = [pl.BlockSpec((None, tm, sp.width), lambda b, i: (b, i, 0)) for sp in splits]
    out_shape = [jax.ShapeDtypeStruct((bsz, rows, sp.width), sp.dtype) for sp in splits]
    out_specs.append(pl.BlockSpec((None, v_width, tm), lambda b, i: (b, 0, i)))
    out_shape.append(jax.ShapeDtypeStruct((bsz, v_width, rows), BF16))
    return pl.pallas_call(
        functools.partial(_inproj_kernel, splits=tuple(splits), use_norm=use_norm, use_rope=use_rope),
        grid=(bsz, rows // tm),
        in_specs=in_specs,
        out_specs=out_specs,
        out_shape=out_shape,
        compiler_params=_params("parallel", "parallel"),
        name="inproj",
    )(*args)


def _half_masks(shape, dtype):
    lane = lax.broadcasted_iota(jnp.int32, shape, 1)
    lo = (lane < HEAD_DIM).astype(dtype)
    return lo, (1 - lo).astype(dtype)


def _softmax_pv_step(state, s, vt):
    cm = jnp.max(s, axis=0, keepdims=True)
    if state is None:
        return cm, _dot(vt, jnp.exp2(s - cm).astype(BF16))
    m, o = state
    m_new = jnp.maximum(m, cm)
    return m_new, jnp.exp2(m - m_new) * o + _dot(vt, jnp.exp2(s - m_new).astype(BF16))


def _attend(groups, k_refs, vt_refs, finish):
    chunks = [(k_ref, vt_ref, slice(c * KEY_CHUNK, (c + 1) * KEY_CHUNK))
              for k_ref, vt_ref in zip(k_refs, vt_refs) for c in range(k_ref.shape[0] // KEY_CHUNK)]
    items = [(g, j) for g in range(len(groups)) for j in range(len(chunks))]

    def scores(item):
        g, j = item
        k_ref, _, ks = chunks[j]
        kc = k_ref[ks, :]
        return [_dot_nt(kc, qm) for qm, _ in groups[g]]

    ones = jnp.ones((BF16_SUBLANES, KEY_CHUNK), BF16)
    pending = [scores(item) for item in items[:QK_LOOKAHEAD]]
    for t, (g, j) in enumerate(items):
        s_cur = pending.pop(0)
        if t + QK_LOOKAHEAD < len(items):
            pending.append(scores(items[t + QK_LOOKAHEAD]))
        if j == 0:
            state = [None] * len(groups[g])
        _, vt_ref, ks = chunks[j]
        state = [_softmax_pv_step(st, s, jnp.concatenate([vt_ref[v_rows, ks], ones], axis=0))
                 for st, s, (_, v_rows) in zip(state, s_cur, groups[g])]
        if j == len(chunks) - 1:
            finish(g, [(o[:-BF16_SUBLANES], o[-BF16_SUBLANES:-BF16_SUBLANES + 1]) for _, o in state])


def _attn_gqa_kernel(q_ref, *refs, n_parts):
    k_refs = refs[:n_parts]
    vt_refs = refs[n_parts:2 * n_parts]
    o_ref = refs[2 * n_parts]
    masks = _half_masks((Q_TILE, LANES), BF16)
    groups, places = [], []
    for r in range(q_ref.shape[0] // Q_TILE):
        for pr in range(q_ref.shape[1] // LANES):
            place = (slice(r * Q_TILE, (r + 1) * Q_TILE), slice(pr * LANES, (pr + 1) * LANES))
            qp = q_ref[place]
            groups.append([(qp * mask, slice(half * HEAD_DIM, (half + 1) * HEAD_DIM))
                           for half, mask in enumerate(masks)])
            places.append(place)

    def finish(g, pair):
        o_pair = jnp.concatenate([o / l for o, l in pair], axis=0)
        o_ref[places[g]] = o_pair.T.astype(o_ref.dtype)

    _attend(groups, k_refs, vt_refs, finish)


def _attn_gqa(q, ks, vts, *, tq):
    bsz, rows, _ = q.shape
    n_parts = len(ks)
    qw = A_WIDTH // 2
    k_specs = [pl.BlockSpec((None, k.shape[1], LANES), lambda b, p, i: (b, 0, p)) for k in ks]
    vt_specs = [pl.BlockSpec((None, LANES, vt.shape[2]), lambda b, p, i: (b, p, 0)) for vt in vts]
    return pl.pallas_call(
        functools.partial(_attn_gqa_kernel, n_parts=n_parts),
        grid=(bsz, 2, rows // tq),
        in_specs=[pl.BlockSpec((None, tq, qw), lambda b, p, i: (b, i, p))] + k_specs + vt_specs,
        out_specs=pl.BlockSpec((None, tq, qw), lambda b, p, i: (b, i, p)),
        out_shape=jax.ShapeDtypeStruct((bsz, rows, A_WIDTH), BF16),
        compiler_params=_params("parallel", "parallel", "parallel"),
        name="attn_gqa",
    )(q, *ks, *vts)


def _attn_diff_kernel(lam_ref, gain_ref, q_ref, *refs, n_parts, lam_init):
    k_refs = refs[:n_parts]
    vt_refs = refs[n_parts:2 * n_parts]
    o_ref = refs[2 * n_parts]
    lv = lam_ref[...]
    lam = (jnp.exp(jnp.sum(lv[0:1] * lv[1:2], axis=1, keepdims=True))
           - jnp.exp(jnp.sum(lv[2:3] * lv[3:4], axis=1, keepdims=True)) + lam_init)
    masks = _half_masks((Q_TILE, LANES), BF16)
    rows = slice(0, C_V_DIM)
    groups = []
    for r in range(q_ref.shape[0] // Q_TILE):
        q = q_ref[r * Q_TILE:(r + 1) * Q_TILE, :]
        groups.append([(q * mask, rows) for mask in masks])

    def finish(g, comps):
        (o0, l0), (o1, l1) = comps
        o = o0 * (1.0 / l0) - o1 * (lam / l1)
        ms = jnp.mean(o * o, axis=0, keepdims=True)
        o = (o * lax.rsqrt(ms + EPS)).T
        o_ref[g * Q_TILE:(g + 1) * Q_TILE, :] = (o * gain_ref[...] * (1.0 - lam_init)).astype(o_ref.dtype)

    _attend(groups, k_refs, vt_refs, finish)


def _attn_diff(lam_vecs, gain, q, ks, vts, *, tq, lam_init):
    bsz, rows, _ = q.shape
    n_parts = len(ks)
    heads = C_WIDTH // C_V_DIM
    k_specs = [pl.BlockSpec((None, k.shape[1], LANES), lambda b, h, i: (b, 0, h)) for k in ks]
    vt_specs = [pl.BlockSpec((None, C_V_DIM, vt.shape[2]), lambda b, h, i: (b, h, 0)) for vt in vts]
    return pl.pallas_call(
        functools.partial(_attn_diff_kernel, n_parts=n_parts, lam_init=lam_init),
        grid=(bsz, heads, rows // tq),
        in_specs=[pl.BlockSpec(lam_vecs.shape, lambda b, h, i: (0, 0)),
                  pl.BlockSpec(gain.shape, lambda b, h, i: (0, 0)),
                  pl.BlockSpec((None, tq, LANES), lambda b, h, i: (b, i, h))] + k_specs + vt_specs,
        out_specs=pl.BlockSpec((None, tq, LANES), lambda b, h, i: (b, i, h)),
        out_shape=jax.ShapeDtypeStruct((bsz, rows, C_WIDTH), BF16),
        compiler_params=_params("parallel", "parallel", "parallel"),
        name="attn_diff",
    )(lam_vecs, gain, q, *ks, *vts)


def _pool_kernel(u_ref, w_ref, ps_ref, o_ref, pad_ref, *, chunk):
    t_len = u_ref.shape[0]
    zeros = jnp.zeros((POOL_PAD, B_WIDTH), F32)
    pad_ref[0:POOL_PAD, :] = zeros
    pad_ref[POOL_PAD + t_len:2 * POOL_PAD + t_len, :] = zeros
    pad_ref[POOL_PAD:POOL_PAD + t_len, :] = u_ref[...]
    grp = lax.broadcasted_iota(jnp.int32, (chunk, B_WIDTH), 1) // B_GROUP_DIM
    for c in range(t_len // chunk):
        r0 = c * chunk

        def win(j):
            return pad_ref[POOL_PAD + r0 + j:POOL_PAD + r0 + j + chunk, :]

        tok = win(0)
        sums = []
        acc = None
        for w in POOL_WINDOWS:
            lo, hi = -(w // 2), w - 1 - w // 2
            if acc is None:
                acc = functools.reduce(jnp.add, [win(j) for j in range(lo, hi + 1)])
            else:
                plo, phi = -(prev_w // 2), prev_w - 1 - prev_w // 2
                acc = functools.reduce(jnp.add, [acc] + [win(j) for j in range(lo, plo)]
                                       + [win(j) for j in range(phi + 1, hi + 1)])
            prev_w = w
            sums.append(acc)
        t = lax.broadcasted_iota(jnp.int32, (chunk, B_WIDTH), 0) + r0
        half = jnp.where(grp == 0, 1, jnp.where(grp == 1, 2, jnp.where(grp == 2, 4, 8)))
        tail = half - 1
        cnt = (jnp.minimum(t + tail + 1, t_len) - jnp.maximum(t - half, 0)).astype(F32)
        s = jnp.where(grp == 0, sums[0], jnp.where(grp == 1, sums[1], jnp.where(grp == 2, sums[2], sums[3])))
        pooled = s / cnt - tok
        mixed = _dot(pooled.astype(BF16), w_ref[...]) * ps_ref[...]
        o_ref[r0:r0 + chunk, :] = mixed.astype(o_ref.dtype)


def _pool(u, w_bd, pool_scale):
    bsz, t_len, _ = u.shape
    return pl.pallas_call(
        functools.partial(_pool_kernel, chunk=min(t_len, 256)),
        grid=(bsz,),
        in_specs=[pl.BlockSpec((None, t_len, B_WIDTH), lambda b: (b, 0, 0)),
                  pl.BlockSpec(w_bd.shape, lambda b: (0, 0)),
                  pl.BlockSpec(pool_scale.shape, lambda b: (0, 0))],
        out_specs=pl.BlockSpec((None, t_len, B_WIDTH), lambda b: (b, 0, 0)),
        out_shape=jax.ShapeDtypeStruct((bsz, t_len, B_WIDTH), BF16),
        scratch_shapes=[pltpu.VMEM((t_len + 2 * POOL_PAD, B_WIDTH), F32)],
        compiler_params=_params("parallel"),
        name="pool_mixer",
    )(u, w_bd, pool_scale)


def _conv_kernel(u_ref, w_ref, b_ref, g_ref, beta_ref, o_ref, pad_ref, *, chunk):
    t_len = u_ref.shape[0]
    zeros = jnp.zeros((CONV_PAD, D_CH), F32)
    pad_ref[0:CONV_PAD, :] = zeros
    pad_ref[CONV_PAD + t_len:2 * CONV_PAD + t_len, :] = zeros
    pad_ref[CONV_PAD:CONV_PAD + t_len, :] = u_ref[:, 0:D_CH] * jax.nn.sigmoid(u_ref[:, D_CH:2 * D_CH])
    half = CONV_WIDTH // 2
    for c in range(t_len // chunk):
        r0 = CONV_PAD + c * chunk - half
        acc = pad_ref[r0:r0 + chunk, :] * w_ref[0:1, :]
        for j in range(1, CONV_WIDTH):
            acc = acc + pad_ref[r0 + j:r0 + j + chunk, :] * w_ref[j:j + 1, :]
        z = _layer_norm(acc + b_ref[...], g_ref[...], beta_ref[...])
        o_ref[c * chunk:(c + 1) * chunk, :] = jax.nn.silu(z).astype(o_ref.dtype)


def _conformer_conv(u, conv_w, conv_b, ln_g, ln_b):
    bsz, t_len, _ = u.shape
    vec = pl.BlockSpec((1, D_CH), lambda b: (0, 0))
    return pl.pallas_call(
        functools.partial(_conv_kernel, chunk=128),
        grid=(bsz,),
        in_specs=[pl.BlockSpec((None, t_len, 2 * D_CH), lambda b: (b, 0, 0)),
                  pl.BlockSpec(conv_w.shape, lambda b: (0, 0)), vec, vec, vec],
        out_specs=pl.BlockSpec((None, t_len, D_CH), lambda b: (b, 0, 0)),
        out_shape=jax.ShapeDtypeStruct((bsz, t_len, D_CH), BF16),
        scratch_shapes=[pltpu.VMEM((t_len + 2 * CONV_PAD, D_CH), F32)],
        compiler_params=_params("parallel"),
        name="conformer_conv",
    )(u, conv_w, conv_b, ln_g, ln_b)


def _outproj_kernel(a_ref, m_ref, w_ref, x_ref, gate_ref, g_ref, b_ref, o_ref):
    wa = a_ref.shape[1]
    y = _dot(a_ref[...], w_ref[0:wa, :]) + _dot(m_ref[...], w_ref[wa:, :])
    z = ALPHA * x_ref[...] + gate_ref[...] * y
    o_ref[...] = _layer_norm(z, g_ref[...], b_ref[...])


def _outproj(attn, mix, w_out, x, mod, ln_g, ln_b, *, tm, per_batch, ctx_row):
    bsz, rows, _ = x.shape
    vec = pl.BlockSpec((1, D_MODEL), lambda b, i: (0, 0))
    return pl.pallas_call(
        _outproj_kernel,
        grid=(bsz, rows // tm),
        in_specs=[pl.BlockSpec((None, tm, attn.shape[2]), lambda b, i: (b, i, 0)),
                  pl.BlockSpec((None, tm, mix.shape[2]), lambda b, i: (b, i, 0)),
                  pl.BlockSpec(w_out.shape, lambda b, i: (0, 0)),
                  pl.BlockSpec((None, tm, D_MODEL), lambda b, i: (b, i, 0)),
                  _mod_spec(2, per_batch, ctx_row), vec, vec],
        out_specs=pl.BlockSpec((None, tm, D_MODEL), lambda b, i: (b, i, 0)),
        out_shape=jax.ShapeDtypeStruct(x.shape, F32),
        compiler_params=_params("parallel", "parallel"),
        name="outproj_ln",
    )(attn, mix, w_out, x, mod, ln_g, ln_b)


def _ffn_kernel(x_ref, sc_ref, sh_ref, gate_ref, wa_ref, wg_ref, wo_ref, g_ref, b_ref, o_ref, h_ref, acc_ref):
    k = pl.program_id(2)

    @pl.when(k == 0)
    def _():
        h_ref[...] = (x_ref[...] * (1.0 + sc_ref[...]) + sh_ref[...]).astype(BF16)
        acc_ref[...] = jnp.zeros_like(acc_ref)

    h = h_ref[...]
    a = _dot(h, wa_ref[...])
    g = _dot(h, wg_ref[...])
    acc_ref[...] += _dot((jax.nn.silu(g) * a).astype(BF16), wo_ref[...])

    @pl.when(k == pl.num_programs(2) - 1)
    def _():
        z = ALPHA * x_ref[...] + gate_ref[...] * acc_ref[...]
        o_ref[...] = _layer_norm(z, g_ref[...], b_ref[...])


def _ffn(x, mod, w_in, w_out, ln_g, ln_b, *, tm, per_batch, ctx_row):
    bsz, rows, _ = x.shape
    n_chunks = FFN_HIDDEN // FFN_CHUNK
    vec = pl.BlockSpec((1, D_MODEL), lambda b, i, k: (0, 0))
    return pl.pallas_call(
        _ffn_kernel,
        grid=(bsz, rows // tm, n_chunks),
        in_specs=[pl.BlockSpec((None, tm, D_MODEL), lambda b, i, k: (b, i, 0)),
                  _mod_spec(4, per_batch, ctx_row),
                  _mod_spec(3, per_batch, ctx_row),
                  _mod_spec(5, per_batch, ctx_row),
                  pl.BlockSpec((D_MODEL, FFN_CHUNK), lambda b, i, k: (0, k)),
                  pl.BlockSpec((D_MODEL, FFN_CHUNK), lambda b, i, k: (0, k + n_chunks)),
                  pl.BlockSpec((FFN_CHUNK, D_MODEL), lambda b, i, k: (k, 0)),
                  vec, vec],
        out_specs=pl.BlockSpec((None, tm, D_MODEL), lambda b, i, k: (b, i, 0)),
        out_shape=jax.ShapeDtypeStruct(x.shape, F32),
        scratch_shapes=[pltpu.VMEM((tm, D_MODEL), BF16), pltpu.VMEM((tm, D_MODEL), F32)],
        compiler_params=_params("parallel", "parallel", "arbitrary"),
        name="ffn_ln",
    )(x, mod, mod, mod, w_in, w_in, w_out, ln_g, ln_b)


def _rope_tables(n_tokens):
    rows = n_tokens // GRID_W
    row = jnp.repeat(jnp.arange(rows, dtype=F32), GRID_W)
    col = jnp.tile(jnp.arange(GRID_W, dtype=F32), rows)
    axis_dim = HEAD_DIM // 2
    freqs = ROPE_THETA ** (-jnp.arange(0, axis_dim, 2, dtype=F32) / axis_dim)
    ang_r, ang_c = row[:, None] * freqs, col[:, None] * freqs
    cos = jnp.concatenate([jnp.cos(ang_r)] * 2 + [jnp.cos(ang_c)] * 2, axis=-1)
    sin = jnp.concatenate([-jnp.sin(ang_r), jnp.sin(ang_r), -jnp.sin(ang_c), jnp.sin(ang_c)], axis=-1)
    return jnp.tile(cos, (1, 2)), jnp.tile(sin, (1, 2))


def _pair_heads_cols(w):
    return w.reshape(w.shape[0], 2, 2, 3, HEAD_DIM).transpose(0, 1, 3, 2, 4).reshape(w.shape[0], A_WIDTH)


def _pair_heads_rows(w):
    return w.reshape(2, 2, 3, HEAD_DIM, w.shape[1]).transpose(0, 2, 1, 3, 4).reshape(A_WIDTH, w.shape[1])


def _block_diag(w_pool):
    out = jnp.zeros((B_WIDTH, B_WIDTH), w_pool.dtype)
    for g in range(B_GROUPS):
        sl = slice(g * B_GROUP_DIM, (g + 1) * B_GROUP_DIM)
        out = out.at[sl, sl].set(w_pool[g])
    return out


def kernel(x, c, ctx, c_ctx, ab_w_in, ab_q_gain, ab_k_gain, ab_w_pool, ab_pool_scale, ab_w_out,
           cd_w_in, cd_lambda_q1, cd_lambda_k1, cd_lambda_q2, cd_lambda_k2, cd_subln_gain,
           cd_conv_w, cd_conv_b, cd_conv_ln_g, cd_conv_ln_b, cd_w_out,
           ada_w, ada_b, ln1_g, ln1_b, ln2_g, ln2_b, ffn_w_in, ffn_w_out):
    bsz, t_len, _ = x.shape
    n_ctx = ctx.shape[1]
    ctx_row = bsz
    mod_rows = -(-(bsz + 1) // 8) * 8
    cc = jnp.zeros((mod_rows, D_MODEL), F32).at[:bsz].set(c).at[bsz].set(c_ctx)
    mods = _modulation(cc, ada_w, ada_b)
    rope = _rope_tables(t_len)
    row = lambda v: v.reshape(1, -1)
    head_of = jnp.arange(MXU_DIM) // HEAD_DIM
    gmat = jnp.where(head_of[:, None] == head_of[None, :], 1.0 / HEAD_DIM, 0.0).astype(BF16)

    l, i = 0, 0
    v0 = A_WIDTH + A_KV_WIDTH
    w_in = jnp.concatenate([_pair_heads_cols(ab_w_in[i][:, :A_WIDTH]), ab_w_in[i][:, A_WIDTH:]], axis=1).astype(BF16)
    w_vt = ab_w_in[i][:, v0:v0 + A_KV_WIDTH].T.astype(BF16)
    w_out = jnp.concatenate([_pair_heads_rows(ab_w_out[i][:A_WIDTH]), ab_w_out[i][A_WIDTH:]], axis=0).astype(BF16)
    gain = jnp.concatenate([jnp.tile(ab_q_gain[i], A_WIDTH // HEAD_DIM),
                            jnp.tile(ab_k_gain[i], A_KV_WIDTH // HEAD_DIM)]).reshape(1, -1)
    splits = [Split(0, A_WIDTH, BF16, Q_SCALE, True, True),
              Split(A_WIDTH, A_KV_WIDTH, BF16, 1.0, True, True),
              Split(v0 + A_KV_WIDTH, B_WIDTH, F32, 1.0, False, False)]
    w_bd = _block_diag(ab_w_pool[i]).astype(BF16)
    pool_scale = row(ab_pool_scale[i])
    w_ffn_in, w_ffn_out = ffn_w_in[l].astype(BF16), ffn_w_out[l].astype(BF16)
    g1, b1, g2, b2 = row(ln1_g[l]), row(ln1_b[l]), row(ln2_g[l]), row(ln2_b[l])

    q, k, u, vt = _inproj(x, mods[l], w_in, w_vt, splits, tm=512, per_batch=True, ctx_row=ctx_row,
                          gain=gain, gmat=gmat, rope=rope)
    qc, kc, uc, vtc = _inproj(ctx, mods[l], w_in, w_vt, splits, tm=n_ctx, per_batch=False, ctx_row=ctx_row,
                              gain=gain, gmat=gmat)
    o = _attn_gqa(q, [kc, k], [vtc, vt], tq=ATTN_TQ)
    oc = _attn_gqa(qc, [kc], [vtc], tq=n_ctx)
    mix = _pool(u, w_bd, pool_scale)
    mixc = _pool(uc, w_bd, pool_scale)
    x = _outproj(o, mix, w_out, x, mods[l], g1, b1, tm=512, per_batch=True, ctx_row=ctx_row)
    xc = _outproj(oc, mixc, w_out, ctx, mods[l], g1, b1, tm=n_ctx, per_batch=False, ctx_row=ctx_row)
    x = _ffn(x, mods[l], w_ffn_in, w_ffn_out, g2, b2, tm=512, per_batch=True, ctx_row=ctx_row)
    xc = _ffn(xc.reshape(1, bsz * n_ctx, D_MODEL), mods[l], w_ffn_in, w_ffn_out, g2, b2,
              tm=512, per_batch=False, ctx_row=ctx_row).reshape(bsz, n_ctx, D_MODEL)

    l, i = 1, 0
    lam_init = 0.8 - 0.6 * math.exp(-0.3 * l)
    w_in = cd_w_in[i].astype(BF16)
    w_vt = cd_w_in[i][:, 2 * C_WIDTH:3 * C_WIDTH].T.astype(BF16)
    w_out = cd_w_out[i].astype(BF16)
    splits = [Split(0, C_WIDTH, BF16, Q_SCALE, False, True),
              Split(C_WIDTH, C_WIDTH, BF16, 1.0, False, True),
              Split(3 * C_WIDTH, 2 * D_CH, F32, 1.0, False, False)]
    ctx_splits = [Split(C_WIDTH, C_WIDTH, BF16, 1.0, False, False)]
    lam_vecs = jnp.stack([cd_lambda_q1[i], cd_lambda_k1[i], cd_lambda_q2[i], cd_lambda_k2[i]])
    sub_gain = row(cd_subln_gain[i])
    w_ffn_in, w_ffn_out = ffn_w_in[l].astype(BF16), ffn_w_out[l].astype(BF16)
    g1, b1, g2, b2 = row(ln1_g[l]), row(ln1_b[l]), row(ln2_g[l]), row(ln2_b[l])

    q, k, u, vt = _inproj(x, mods[l], w_in, w_vt, splits, tm=512, per_batch=True, ctx_row=ctx_row, rope=rope)
    kc, vtc = _inproj(xc, mods[l], w_in, w_vt, ctx_splits, tm=n_ctx, per_batch=False, ctx_row=ctx_row)
    o = _attn_diff(lam_vecs, sub_gain, q, [kc, k], [vtc, vt], tq=ATTN_TQ, lam_init=lam_init)
    conv = _conformer_conv(u, cd_conv_w[i], row(cd_conv_b[i]), row(cd_conv_ln_g[i]), row(cd_conv_ln_b[i]))
    x = _outproj(o, conv, w_out, x, mods[l], g1, b1, tm=512, per_batch=True, ctx_row=ctx_row)
    x = _ffn(x, mods[l], w_ffn_in, w_ffn_out, g2, b2, tm=512, per_batch=True, ctx_row=ctx_row)
    return x
```

```python
import functools
import math
from typing import NamedTuple

import jax
import jax.numpy as jnp
from jax import lax
from jax.experimental import pallas as pl
from jax.experimental.pallas import tpu as pltpu

F32 = jnp.float32
BF16 = jnp.bfloat16

D_MODEL = 1024
DEPTH = 2
GRID_W = 64
HEAD_DIM = 64
ROPE_THETA = 10000.0
EPS = 1e-6

B_WIDTH = D_MODEL // 4
B_GROUPS = 4
B_GROUP_DIM = B_WIDTH // B_GROUPS
POOL_WINDOWS = (2, 4, 8, 16)
A_WIDTH = D_MODEL - B_WIDTH
A_KV_WIDTH = 4 * HEAD_DIM
D_CH = D_MODEL // 4
C_WIDTH = D_MODEL - D_CH
C_V_DIM = 2 * HEAD_DIM
CONV_WIDTH = 31
FFN_HIDDEN = -(-8 * D_MODEL // (3 * 256)) * 256
ALPHA = (2.0 * DEPTH) ** 0.25

LANES = 128
BF16_SUBLANES = 16
Q_SCALE = HEAD_DIM ** -0.5 * math.log2(math.e)
MXU_DIM = 256
VMEM_LIMIT = 48 * 1024 * 1024

POOL_PAD = 8
CONV_PAD = 16
FFN_CHUNK = 256
FFN_LOOKAHEAD = 1
KEY_CHUNK = MXU_DIM
QK_LOOKAHEAD = 2
ROW_TILE = MXU_DIM
PROJ_TM = 1024
FFN_TM = 512
Q_TILE = MXU_DIM
ATTN_TQ = 1024


def _params(*semantics):
    return pltpu.CompilerParams(dimension_semantics=semantics, vmem_limit_bytes=VMEM_LIMIT)


def _layer_norm(z, g, b):
    mu = jnp.mean(z, axis=-1, keepdims=True)
    zc = z - mu
    var = jnp.mean(zc * zc, axis=-1, keepdims=True)
    return zc * lax.rsqrt(var + EPS) * g + b


def _dot(a, b):
    return jnp.dot(a, b, preferred_element_type=F32)


def _dot_nt(a, b):
    return lax.dot_general(a, b, (((1,), (1,)), ((), ())), preferred_element_type=F32)


def _mod_kernel(c_ref, w_ref, b_ref, o_ref):
    s = jax.nn.silu(c_ref[...])
    o_ref[...] = jnp.dot(s, w_ref[...], preferred_element_type=F32,
                         precision=lax.Precision.HIGHEST) + b_ref[...]


def _modulation(cc, ada_w, ada_b):
    rows = cc.shape[0]
    tn = 1536
    out = pl.pallas_call(
        _mod_kernel,
        grid=(DEPTH, 6 * D_MODEL // tn),
        in_specs=[
            pl.BlockSpec((rows, D_MODEL), lambda l, j: (0, 0)),
            pl.BlockSpec((None, D_MODEL, tn), lambda l, j: (l, 0, j)),
            pl.BlockSpec((None, 1, tn), lambda l, j: (l, 0, j)),
        ],
        out_specs=pl.BlockSpec((None, rows, tn), lambda l, j: (l, 0, j)),
        out_shape=jax.ShapeDtypeStruct((DEPTH, rows, 6 * D_MODEL), F32),
        compiler_params=_params("parallel", "parallel"),
        name="adaln_mod",
    )(cc, ada_w, ada_b.reshape(DEPTH, 1, 6 * D_MODEL))
    return out.reshape(DEPTH, rows, 1, 6 * D_MODEL)


def _mod_spec(chunk, per_batch, ctx_row):
    if per_batch:
        return pl.BlockSpec((None, 1, D_MODEL), lambda b, *_: (b, 0, chunk))
    return pl.BlockSpec((None, 1, D_MODEL), lambda b, *_: (ctx_row, 0, chunk))


class Split(NamedTuple):
    start: int
    width: int
    dtype: object
    scale: float
    norm: bool
    rope: bool


def _group_rms_norm(p, gain, gmat):
    ss = p * p
    hi = ss.astype(BF16)
    lo = (ss - hi.astype(F32)).astype(BF16)
    blocks = []
    for j in range(p.shape[1] // MXU_DIM):
        sl = slice(j * MXU_DIM, (j + 1) * MXU_DIM)
        blocks.append(_dot(hi[:, sl], gmat) + _dot(lo[:, sl], gmat))
    ms = blocks[0] if len(blocks) == 1 else jnp.concatenate(blocks, axis=1)
    return p * lax.rsqrt(ms + EPS) * gain


def _rope(p, cos, sin):
    tm = p.shape[0]
    lane = lax.broadcasted_iota(jnp.int32, (tm, LANES), 1)
    first = (lane & 16) == 0
    outs = []
    for j in range(p.shape[1] // LANES):
        xb = p[:, j * LANES:(j + 1) * LANES]
        partner = jnp.where(first, pltpu.roll(xb, LANES - 16, 1), pltpu.roll(xb, 16, 1))
        outs.append(xb * cos + partner * sin)
    return jnp.concatenate(outs, axis=1)


def _inproj_kernel(x_ref, sc_ref, sh_ref, w_ref, wvt_ref, *rest, splits, use_norm, use_rope):
    idx = 0
    if use_norm:
        gain_ref, gmat_ref = rest[0], rest[1]
        idx = 2
    if use_rope:
        cos_ref, sin_ref = rest[idx], rest[idx + 1]
        idx += 2
    out_refs = rest[idx:-1]
    vt_ref = rest[-1]
    sub = min(ROW_TILE, x_ref.shape[0])
    n_sub = x_ref.shape[0] // sub

    def project(r):
        rs = slice(r * sub, (r + 1) * sub)
        h = (x_ref[rs, :] * (1.0 + sc_ref[...]) + sh_ref[...]).astype(BF16)
        vt_ref[:, rs] = _dot_nt(wvt_ref[...], h).astype(vt_ref.dtype)
        return [_dot(h, w_ref[:, sp.start:sp.start + sp.width]) for sp in splits]

    def finish(r, ps):
        rs = slice(r * sub, (r + 1) * sub)
        gain_off = 0
        for sp, p, o_ref in zip(splits, ps, out_refs):
            if sp.norm:
                p = _group_rms_norm(p, gain_ref[:, gain_off:gain_off + sp.width], gmat_ref[...])
                gain_off += sp.width
            if sp.rope and use_rope:
                p = _rope(p, cos_ref[rs, :], sin_ref[rs, :])
            if sp.scale != 1.0:
                p = p * sp.scale
            o_ref[rs, :] = p.astype(o_ref.dtype)

    ps_next = project(0)
    for r in range(n_sub):
        ps = ps_next
        if r + 1 < n_sub:
            ps_next = project(r + 1)
        finish(r, ps)


def _inproj(x, mod, w, wvt, splits, *, tm, per_batch, ctx_row, gain=None, gmat=None, rope=None):
    bsz, rows, _ = x.shape
    n_cols = w.shape[1]
    v_width = wvt.shape[0]
    use_norm = gain is not None
    use_rope = rope is not None
    in_specs = [
        pl.BlockSpec((None, tm, D_MODEL), lambda b, i: (b, i, 0)),
        _mod_spec(1, per_batch, ctx_row),
        _mod_spec(0, per_batch, ctx_row),
        pl.BlockSpec((D_MODEL, n_cols), lambda b, i: (0, 0)),
        pl.BlockSpec((v_width, D_MODEL), lambda b, i: (0, 0)),
    ]
    args = [x, mod, mod, w, wvt]
    if use_norm:
        in_specs += [pl.BlockSpec(gain.shape, lambda b, i: (0, 0)),
                     pl.BlockSpec(gmat.shape, lambda b, i: (0, 0))]
        args += [gain, gmat]
    if use_rope:
        in_specs += [pl.BlockSpec((tm, LANES), lambda b, i: (i, 0))] * 2
        args += [rope[0], rope[1]]
    out_specs = [pl.BlockSpec((None, tm, sp.width), lambda b, i: (b, i, 0)) for sp in splits]
    out_shape = [jax.ShapeDtypeStruct((bsz, rows, sp.width), sp.dtype) for sp in splits]
    out_specs.append(pl.BlockSpec((None, v_width, tm), lambda b, i: (b, 0, i)))
    out_shape.append(jax.ShapeDtypeStruct((bsz, v_width, rows), BF16))
    return pl.pallas_call(
        functools.partial(_inproj_kernel, splits=tuple(splits), use_norm=use_norm, use_rope=use_rope),
        grid=(bsz, rows // tm),
        in_specs=in_specs,
        out_specs=out_specs,
        out_shape=out_shape,
        compiler_params=_params("parallel", "parallel"),
        name="inproj",
    )(*args)


def _half_masks(shape, dtype):
    lane = lax.broadcasted_iota(jnp.int32, shape, 1)
    lo = (lane < HEAD_DIM).astype(dtype)
    return lo, (1 - lo).astype(dtype)


def _softmax_pv_step(state, s, vt):
    cm = jnp.max(s, axis=0, keepdims=True)
    if state is None:
        return cm, _dot(vt, jnp.exp2(s - cm).astype(BF16))
    m, o = state
    m_new = jnp.maximum(m, cm)
    return m_new, jnp.exp2(m - m_new) * o + _dot(vt, jnp.exp2(s - m_new).astype(BF16))


def _attend(groups, k_refs, vt_refs, finish):
    chunks = [(k_ref, vt_ref, slice(c * KEY_CHUNK, (c + 1) * KEY_CHUNK))
              for k_ref, vt_ref in zip(k_refs, vt_refs) for c in range(k_ref.shape[0] // KEY_CHUNK)]
    items = [(g, j) for g in range(len(groups)) for j in range(len(chunks))]

    def scores(item):
        g, j = item
        k_ref, _, ks = chunks[j]
        kc = k_ref[ks, :]
        return [_dot_nt(kc, qm) for qm, _ in groups[g]]

    ones = jnp.ones((BF16_SUBLANES, KEY_CHUNK), BF16)
    pending = [scores(item) for item in items[:QK_LOOKAHEAD]]
    for t, (g, j) in enumerate(items):
        s_cur = pending.pop(0)
        if t + QK_LOOKAHEAD < len(items):
            pending.append(scores(items[t + QK_LOOKAHEAD]))
        if j == 0:
            state = [None] * len(groups[g])
        _, vt_ref, ks = chunks[j]
        state = [_softmax_pv_step(st, s, jnp.concatenate([vt_ref[v_rows, ks], ones], axis=0))
                 for st, s, (_, v_rows) in zip(state, s_cur, groups[g])]
        if j == len(chunks) - 1:
            finish(g, [(o[:-BF16_SUBLANES], o[-BF16_SUBLANES:-BF16_SUBLANES + 1]) for _, o in state])


def _attn_gqa_kernel(q_ref, *refs, n_parts):
    k_refs = refs[:n_parts]
    vt_refs = refs[n_parts:2 * n_parts]
    o_ref = refs[2 * n_parts]
    masks = _half_masks((Q_TILE, LANES), BF16)
    groups, places = [], []
    for r in range(q_ref.shape[0] // Q_TILE):
        for pr in range(q_ref.shape[1] // LANES):
            place = (slice(r * Q_TILE, (r + 1) * Q_TILE), slice(pr * LANES, (pr + 1) * LANES))
            qp = q_ref[place]
            groups.append([(qp * mask, slice(half * HEAD_DIM, (half + 1) * HEAD_DIM))
                           for half, mask in enumerate(masks)])
            places.append(place)

    def finish(g, pair):
        o_pair = jnp.concatenate([o / l for o, l in pair], axis=0)
        o_ref[places[g]] = o_pair.T.astype(o_ref.dtype)

    _attend(groups, k_refs, vt_refs, finish)


def _attn_gqa(q, ks, vts, *, tq):
    bsz, rows, _ = q.shape
    n_parts = len(ks)
    qw = A_WIDTH // 2
    k_specs = [pl.BlockSpec((None, k.shape[1], LANES), lambda b, p, i: (b, 0, p)) for k in ks]
    vt_specs = [pl.BlockSpec((None, LANES, vt.shape[2]), lambda b, p, i: (b, p, 0)) for vt in vts]
    return pl.pallas_call(
        functools.partial(_attn_gqa_kernel, n_parts=n_parts),
        grid=(bsz, 2, rows // tq),
        in_specs=[pl.BlockSpec((None, tq, qw), lambda b, p, i: (b, i, p))] + k_specs + vt_specs,
        out_specs=pl.BlockSpec((None, tq, qw), lambda b, p, i: (b, i, p)),
        out_shape=jax.ShapeDtypeStruct((bsz, rows, A_WIDTH), BF16),
        compiler_params=_params("parallel", "parallel", "parallel"),
        name="attn_gqa",
    )(q, *ks, *vts)


def _attn_diff_kernel(lam_ref, gain_ref, q_ref, *refs, n_parts, lam_init):
    k_refs = refs[:n_parts]
    vt_refs = refs[n_parts:2 * n_parts]
    o_ref = refs[2 * n_parts]
    lv = lam_ref[...]
    lam = (jnp.exp(jnp.sum(lv[0:1] * lv[1:2], axis=1, keepdims=True))
           - jnp.exp(jnp.sum(lv[2:3] * lv[3:4], axis=1, keepdims=True)) + lam_init)
    masks = _half_masks((Q_TILE, LANES), BF16)
    rows = slice(0, C_V_DIM)
    groups = []
    for r in range(q_ref.shape[0] // Q_TILE):
        q = q_ref[r * Q_TILE:(r + 1) * Q_TILE, :]
        groups.append([(q * mask, rows) for mask in masks])

    def finish(g, comps):
        (o0, l0), (o1, l1) = comps
        o = o0 * (1.0 / l0) - o1 * (lam / l1)
        ms = jnp.mean(o * o, axis=0, keepdims=True)
        o = (o * lax.rsqrt(ms + EPS)).T
        o_ref[g * Q_TILE:(g + 1) * Q_TILE, :] = (o * gain_ref[...] * (1.0 - lam_init)).astype(o_ref.dtype)

    _attend(groups, k_refs, vt_refs, finish)


def _attn_diff(lam_vecs, gain, q, ks, vts, *, tq, lam_init):
    bsz, rows, _ = q.shape
    n_parts = len(ks)
    heads = C_WIDTH // C_V_DIM
    k_specs = [pl.BlockSpec((None, k.shape[1], LANES), lambda b, h, i: (b, 0, h)) for k in ks]
    vt_specs = [pl.BlockSpec((None, C_V_DIM, vt.shape[2]), lambda b, h, i: (b, h, 0)) for vt in vts]
    return pl.pallas_call(
        functools.partial(_attn_diff_kernel, n_parts=n_parts, lam_init=lam_init),
        grid=(bsz, heads, rows // tq),
        in_specs=[pl.BlockSpec(lam_vecs.shape, lambda b, h, i: (0, 0)),
                  pl.BlockSpec(gain.shape, lambda b, h, i: (0, 0)),
                  pl.BlockSpec((None, tq, LANES), lambda b, h, i: (b, i, h))] + k_specs + vt_specs,
        out_specs=pl.BlockSpec((None, tq, LANES), lambda b, h, i: (b, i, h)),
        out_shape=jax.ShapeDtypeStruct((bsz, rows, C_WIDTH), BF16),
        compiler_params=_params("parallel", "parallel", "parallel"),
        name="attn_diff",
    )(lam_vecs, gain, q, *ks, *vts)


def _pool_kernel(u_ref, w_ref, ps_ref, o_ref, pad_ref, *, chunk):
    t_len = u_ref.shape[0]
    zeros = jnp.zeros((POOL_PAD, B_WIDTH), F32)
    pad_ref[0:POOL_PAD, :] = zeros
    pad_ref[POOL_PAD + t_len:2 * POOL_PAD + t_len, :] = zeros
    pad_ref[POOL_PAD:POOL_PAD + t_len, :] = u_ref[...]
    grp = lax.broadcasted_iota(jnp.int32, (chunk, B_WIDTH), 1) // B_GROUP_DIM
    for c in range(t_len // chunk):
        r0 = c * chunk

        def win(j):
            return pad_ref[POOL_PAD + r0 + j:POOL_PAD + r0 + j + chunk, :]

        tok = win(0)
        sums = []
        acc = None
        for w in POOL_WINDOWS:
            lo, hi = -(w // 2), w - 1 - w // 2
            if acc is None:
                acc = functools.reduce(jnp.add, [win(j) for j in range(lo, hi + 1)])
            else:
                plo, phi = -(prev_w // 2), prev_w - 1 - prev_w // 2
                acc = functools.reduce(jnp.add, [acc] + [win(j) for j in range(lo, plo)]
                                       + [win(j) for j in range(phi + 1, hi + 1)])
            prev_w = w
            sums.append(acc)
        t = lax.broadcasted_iota(jnp.int32, (chunk, B_WIDTH), 0) + r0
        half = jnp.where(grp == 0, 1, jnp.where(grp == 1, 2, jnp.where(grp == 2, 4, 8)))
        tail = half - 1
        cnt = (jnp.minimum(t + tail + 1, t_len) - jnp.maximum(t - half, 0)).astype(F32)
        s = jnp.where(grp == 0, sums[0], jnp.where(grp == 1, sums[1], jnp.where(grp == 2, sums[2], sums[3])))
        pooled = s / cnt - tok
        mixed = _dot(pooled.astype(BF16), w_ref[...]) * ps_ref[...]
        o_ref[r0:r0 + chunk, :] = mixed.astype(o_ref.dtype)


def _pool(u, w_bd, pool_scale):
    bsz, t_len, _ = u.shape
    return pl.pallas_call(
        functools.partial(_pool_kernel, chunk=min(t_len, 256)),
        grid=(bsz,),
        in_specs=[pl.BlockSpec((None, t_len, B_WIDTH), lambda b: (b, 0, 0)),
                  pl.BlockSpec(w_bd.shape, lambda b: (0, 0)),
                  pl.BlockSpec(pool_scale.shape, lambda b: (0, 0))],
        out_specs=pl.BlockSpec((None, t_len, B_WIDTH), lambda b: (b, 0, 0)),
        out_shape=jax.ShapeDtypeStruct((bsz, t_len, B_WIDTH), BF16),
        scratch_shapes=[pltpu.VMEM((t_len + 2 * POOL_PAD, B_WIDTH), F32)],
        compiler_params=_params("parallel"),
        name="pool_mixer",
    )(u, w_bd, pool_scale)


def _conv_kernel(u_ref, w_ref, b_ref, g_ref, beta_ref, o_ref, pad_ref, *, chunk):
    t_len = u_ref.shape[0]
    zeros = jnp.zeros((CONV_PAD, D_CH), F32)
    pad_ref[0:CONV_PAD, :] = zeros
    pad_ref[CONV_PAD + t_len:2 * CONV_PAD + t_len, :] = zeros
    pad_ref[CONV_PAD:CONV_PAD + t_len, :] = u_ref[:, 0:D_CH] * jax.nn.sigmoid(u_ref[:, D_CH:2 * D_CH])
    half = CONV_WIDTH // 2
    for c in range(t_len // chunk):
        r0 = CONV_PAD + c * chunk - half
        acc = pad_ref[r0:r0 + chunk, :] * w_ref[0:1, :]
        for j in range(1, CONV_WIDTH):
            acc = acc + pad_ref[r0 + j:r0 + j + chunk, :] * w_ref[j:j + 1, :]
        z = _layer_norm(acc + b_ref[...], g_ref[...], beta_ref[...])
        o_ref[c * chunk:(c + 1) * chunk, :] = jax.nn.silu(z).astype(o_ref.dtype)


def _conformer_conv(u, conv_w, conv_b, ln_g, ln_b):
    bsz, t_len, _ = u.shape
    vec = pl.BlockSpec((1, D_CH), lambda b: (0, 0))
    return pl.pallas_call(
        functools.partial(_conv_kernel, chunk=128),
        grid=(bsz,),
        in_specs=[pl.BlockSpec((None, t_len, 2 * D_CH), lambda b: (b, 0, 0)),
                  pl.BlockSpec(conv_w.shape, lambda b: (0, 0)), vec, vec, vec],
        out_specs=pl.BlockSpec((None, t_len, D_CH), lambda b: (b, 0, 0)),
        out_shape=jax.ShapeDtypeStruct((bsz, t_len, D_CH), BF16),
        scratch_shapes=[pltpu.VMEM((t_len + 2 * CONV_PAD, D_CH), F32)],
        compiler_params=_params("parallel"),
        name="conformer_conv",
    )(u, conv_w, conv_b, ln_g, ln_b)


def _outproj_kernel(a_ref, m_ref, w_ref, x_ref, gate_ref, g_ref, b_ref, o_ref):
    wa = a_ref.shape[1]
    sub = min(ROW_TILE, x_ref.shape[0])
    n_sub = x_ref.shape[0] // sub

    def project(r):
        rs = slice(r * sub, (r + 1) * sub)
        return _dot(a_ref[rs, :], w_ref[0:wa, :]) + _dot(m_ref[rs, :], w_ref[wa:, :])

    y_next = project(0)
    for r in range(n_sub):
        y = y_next
        if r + 1 < n_sub:
            y_next = project(r + 1)
        rs = slice(r * sub, (r + 1) * sub)
        z = ALPHA * x_ref[rs, :] + gate_ref[...] * y
        o_ref[rs, :] = _layer_norm(z, g_ref[...], b_ref[...])


def _outproj(attn, mix, w_out, x, mod, ln_g, ln_b, *, tm, per_batch, ctx_row):
    bsz, rows, _ = x.shape
    vec = pl.BlockSpec((1, D_MODEL), lambda b, i: (0, 0))
    return pl.pallas_call(
        _outproj_kernel,
        grid=(bsz, rows // tm),
        in_specs=[pl.BlockSpec((None, tm, attn.shape[2]), lambda b, i: (b, i, 0)),
                  pl.BlockSpec((None, tm, mix.shape[2]), lambda b, i: (b, i, 0)),
                  pl.BlockSpec(w_out.shape, lambda b, i: (0, 0)),
                  pl.BlockSpec((None, tm, D_MODEL), lambda b, i: (b, i, 0)),
                  _mod_spec(2, per_batch, ctx_row), vec, vec],
        out_specs=pl.BlockSpec((None, tm, D_MODEL), lambda b, i: (b, i, 0)),
        out_shape=jax.ShapeDtypeStruct(x.shape, F32),
        compiler_params=_params("parallel", "parallel"),
        name="outproj_ln",
    )(attn, mix, w_out, x, mod, ln_g, ln_b)


def _ffn_kernel(x_ref, sc_ref, sh_ref, gate_ref, wi_ref, wo_ref, g_ref, b_ref, o_ref, acc_ref):
    h = (x_ref[...] * (1.0 + sc_ref[...]) + sh_ref[...]).astype(BF16)
    n_chunks = FFN_HIDDEN // FFN_CHUNK

    def up(c):
        lo = c * FFN_CHUNK
        return (_dot(h, wi_ref[:, lo:lo + FFN_CHUNK]),
                _dot(h, wi_ref[:, FFN_HIDDEN + lo:FFN_HIDDEN + lo + FFN_CHUNK]))

    pending = [up(c) for c in range(FFN_LOOKAHEAD)]
    for c in range(n_chunks):
        a, g = pending.pop(0)
        if c + FFN_LOOKAHEAD < n_chunks:
            pending.append(up(c + FFN_LOOKAHEAD))
        y = _dot((jax.nn.silu(g) * a).astype(BF16), wo_ref[c * FFN_CHUNK:(c + 1) * FFN_CHUNK, :])
        if c == 0:
            acc_ref[...] = y
        else:
            acc_ref[...] += y
    z = ALPHA * x_ref[...] + gate_ref[...] * acc_ref[...]
    o_ref[...] = _layer_norm(z, g_ref[...], b_ref[...])


def _ffn(x, mod, w_in, w_out, ln_g, ln_b, *, tm, per_batch, ctx_row):
    bsz, rows, _ = x.shape
    vec = pl.BlockSpec((1, D_MODEL), lambda b, i: (0, 0))
    resident = dict(pipeline_mode=pl.Buffered(1))
    return pl.pallas_call(
        _ffn_kernel,
        grid=(bsz, rows // tm),
        in_specs=[pl.BlockSpec((None, tm, D_MODEL), lambda b, i: (b, i, 0)),
                  _mod_spec(4, per_batch, ctx_row),
                  _mod_spec(3, per_batch, ctx_row),
                  _mod_spec(5, per_batch, ctx_row),
                  pl.BlockSpec(w_in.shape, lambda b, i: (0, 0), **resident),
                  pl.BlockSpec(w_out.shape, lambda b, i: (0, 0), **resident),
                  vec, vec],
        out_specs=pl.BlockSpec((None, tm, D_MODEL), lambda b, i: (b, i, 0)),
        out_shape=jax.ShapeDtypeStruct(x.shape, F32),
        scratch_shapes=[pltpu.VMEM((tm, D_MODEL), F32)],
        compiler_params=_params("parallel", "parallel"),
        name="ffn_ln",
    )(x, mod, mod, mod, w_in, w_out, ln_g, ln_b)


def _rope_tables(n_tokens):
    rows = n_tokens // GRID_W
    row = jnp.repeat(jnp.arange(rows, dtype=F32), GRID_W)
    col = jnp.tile(jnp.arange(GRID_W, dtype=F32), rows)
    axis_dim = HEAD_DIM // 2
    freqs = ROPE_THETA ** (-jnp.arange(0, axis_dim, 2, dtype=F32) / axis_dim)
    ang_r, ang_c = row[:, None] * freqs, col[:, None] * freqs
    cos = jnp.concatenate([jnp.cos(ang_r)] * 2 + [jnp.cos(ang_c)] * 2, axis=-1)
    sin = jnp.concatenate([-jnp.sin(ang_r), jnp.sin(ang_r), -jnp.sin(ang_c), jnp.sin(ang_c)], axis=-1)
    return jnp.tile(cos, (1, 2)), jnp.tile(sin, (1, 2))


def _pair_heads_cols(w):
    return w.reshape(w.shape[0], 2, 2, 3, HEAD_DIM).transpose(0, 1, 3, 2, 4).reshape(w.shape[0], A_WIDTH)


def _pair_heads_rows(w):
    return w.reshape(2, 2, 3, HEAD_DIM, w.shape[1]).transpose(0, 2, 1, 3, 4).reshape(A_WIDTH, w.shape[1])


def _block_diag(w_pool):
    out = jnp.zeros((B_WIDTH, B_WIDTH), w_pool.dtype)
    for g in range(B_GROUPS):
        sl = slice(g * B_GROUP_DIM, (g + 1) * B_GROUP_DIM)
        out = out.at[sl, sl].set(w_pool[g])
    return out


def kernel(x, c, ctx, c_ctx, ab_w_in, ab_q_gain, ab_k_gain, ab_w_pool, ab_pool_scale, ab_w_out,
           cd_w_in, cd_lambda_q1, cd_lambda_k1, cd_lambda_q2, cd_lambda_k2, cd_subln_gain,
           cd_conv_w, cd_conv_b, cd_conv_ln_g, cd_conv_ln_b, cd_w_out,
           ada_w, ada_b, ln1_g, ln1_b, ln2_g, ln2_b, ffn_w_in, ffn_w_out):
    bsz, t_len, _ = x.shape
    n_ctx = ctx.shape[1]
    ctx_row = bsz
    mod_rows = -(-(bsz + 1) // 8) * 8
    cc = jnp.zeros((mod_rows, D_MODEL), F32).at[:bsz].set(c).at[bsz].set(c_ctx)
    mods = _modulation(cc, ada_w, ada_b)
    rope = _rope_tables(t_len)
    row = lambda v: v.reshape(1, -1)
    head_of = jnp.arange(MXU_DIM) // HEAD_DIM
    gmat = jnp.where(head_of[:, None] == head_of[None, :], 1.0 / HEAD_DIM, 0.0).astype(BF16)

    l, i = 0, 0
    v0 = A_WIDTH + A_KV_WIDTH
    w_in = jnp.concatenate([_pair_heads_cols(ab_w_in[i][:, :A_WIDTH]), ab_w_in[i][:, A_WIDTH:]], axis=1).astype(BF16)
    w_vt = ab_w_in[i][:, v0:v0 + A_KV_WIDTH].T.astype(BF16)
    w_out = jnp.concatenate([_pair_heads_rows(ab_w_out[i][:A_WIDTH]), ab_w_out[i][A_WIDTH:]], axis=0).astype(BF16)
    gain = jnp.concatenate([jnp.tile(ab_q_gain[i], A_WIDTH // HEAD_DIM),
                            jnp.tile(ab_k_gain[i], A_KV_WIDTH // HEAD_DIM)]).reshape(1, -1)
    splits = [Split(0, A_WIDTH, BF16, Q_SCALE, True, True),
              Split(A_WIDTH, A_KV_WIDTH, BF16, 1.0, True, True),
              Split(v0 + A_KV_WIDTH, B_WIDTH, F32, 1.0, False, False)]
    w_bd = _block_diag(ab_w_pool[i]).astype(BF16)
    pool_scale = row(ab_pool_scale[i])
    w_ffn_in, w_ffn_out = ffn_w_in[l].astype(BF16), ffn_w_out[l].astype(BF16)
    g1, b1, g2, b2 = row(ln1_g[l]), row(ln1_b[l]), row(ln2_g[l]), row(ln2_b[l])

    q, k, u, vt = _inproj(x, mods[l], w_in, w_vt, splits, tm=PROJ_TM, per_batch=True, ctx_row=ctx_row,
                          gain=gain, gmat=gmat, rope=rope)
    qc, kc, uc, vtc = _inproj(ctx, mods[l], w_in, w_vt, splits, tm=n_ctx, per_batch=False, ctx_row=ctx_row,
                              gain=gain, gmat=gmat)
    o = _attn_gqa(q, [kc, k], [vtc, vt], tq=ATTN_TQ)
    oc = _attn_gqa(qc, [kc], [vtc], tq=n_ctx)
    mix = _pool(u, w_bd, pool_scale)
    mixc = _pool(uc, w_bd, pool_scale)
    x = _outproj(o, mix, w_out, x, mods[l], g1, b1, tm=PROJ_TM, per_batch=True, ctx_row=ctx_row)
    xc = _outproj(oc, mixc, w_out, ctx, mods[l], g1, b1, tm=n_ctx, per_batch=False, ctx_row=ctx_row)
    x = _ffn(x, mods[l], w_ffn_in, w_ffn_out, g2, b2, tm=FFN_TM, per_batch=True, ctx_row=ctx_row)
    xc = _ffn(xc.reshape(1, bsz * n_ctx, D_MODEL), mods[l], w_ffn_in, w_ffn_out, g2, b2,
              tm=FFN_TM, per_batch=False, ctx_row=ctx_row).reshape(bsz, n_ctx, D_MODEL)

    l, i = 1, 0
    lam_init = 0.8 - 0.6 * math.exp(-0.3 * l)
    w_in = cd_w_in[i].astype(BF16)
    w_vt = cd_w_in[i][:, 2 * C_WIDTH:3 * C_WIDTH].T.astype(BF16)
    w_out = cd_w_out[i].astype(BF16)
    splits = [Split(0, C_WIDTH, BF16, Q_SCALE, False, True),
              Split(C_WIDTH, C_WIDTH, BF16, 1.0, False, True),
              Split(3 * C_WIDTH, 2 * D_CH, F32, 1.0, False, False)]
    ctx_splits = [Split(C_WIDTH, C_WIDTH, BF16, 1.0, False, False)]
    lam_vecs = jnp.stack([cd_lambda_q1[i], cd_lambda_k1[i], cd_lambda_q2[i], cd_lambda_k2[i]])
    sub_gain = row(cd_subln_gain[i])
    w_ffn_in, w_ffn_out = ffn_w_in[l].astype(BF16), ffn_w_out[l].astype(BF16)
    g1, b1, g2, b2 = row(ln1_g[l]), row(ln1_b[l]), row(ln2_g[l]), row(ln2_b[l])

    q, k, u, vt = _inproj(x, mods[l], w_in, w_vt, splits, tm=PROJ_TM, per_batch=True, ctx_row=ctx_row, rope=rope)
    kc, vtc = _inproj(xc, mods[l], w_in, w_vt, ctx_splits, tm=n_ctx, per_batch=False, ctx_row=ctx_row)
    o = _attn_diff(lam_vecs, sub_gain, q, [kc, k], [vtc, vt], tq=ATTN_TQ, lam_init=lam_init)
    conv = _conformer_conv(u, cd_conv_w[i], row(cd_conv_b[i]), row(cd_conv_ln_g[i]), row(cd_conv_ln_b[i]))
    x = _outproj(o, conv, w_out, x, mods[l], g1, b1, tm=PROJ_TM, per_batch=True, ctx_row=ctx_row)
    x = _ffn(x, mods[l], w_ffn_in, w_ffn_out, g2, b2, tm=FFN_TM, per_batch=True, ctx_row=ctx_row)
    return x
```

```python
import functools
import math
from typing import NamedTuple

import jax
import jax.numpy as jnp
from jax import lax
from jax.experimental import pallas as pl
from jax.experimental.pallas import tpu as pltpu

F32 = jnp.float32
BF16 = jnp.bfloat16

D_MODEL = 1024
DEPTH = 2
GRID_W = 64
HEAD_DIM = 64
ROPE_THETA = 10000.0
EPS = 1e-6

B_WIDTH = D_MODEL // 4
B_GROUPS = 4
B_GROUP_DIM = B_WIDTH // B_GROUPS
POOL_WINDOWS = (2, 4, 8, 16)
A_WIDTH = D_MODEL - B_WIDTH
A_KV_WIDTH = 4 * HEAD_DIM
D_CH = D_MODEL // 4
C_WIDTH = D_MODEL - D_CH
C_V_DIM = 2 * HEAD_DIM
CONV_WIDTH = 31
FFN_HIDDEN = -(-8 * D_MODEL // (3 * 256)) * 256
ALPHA = (2.0 * DEPTH) ** 0.25

LANES = 128
SUBLANES = 8
BF16_SUBLANES = 16
Q_SCALE = HEAD_DIM ** -0.5 * math.log2(math.e)
MXU_DIM = 256
VMEM_LIMIT = 48 * 1024 * 1024

POOL_PAD = 8
CONV_PAD = 16
FFN_CHUNK = 256
FFN_LOOKAHEAD = 1
KEY_CHUNK = MXU_DIM
QK_LOOKAHEAD = 2
ROW_TILE = MXU_DIM
PROJ_TM = 1024
FFN_TM = 512
Q_TILE = MXU_DIM
ATTN_TQ = 1024


def _params(*semantics):
    return pltpu.CompilerParams(dimension_semantics=semantics, vmem_limit_bytes=VMEM_LIMIT)


def _layer_norm(z, g, b):
    mu = jnp.mean(z, axis=-1, keepdims=True)
    zc = z - mu
    var = jnp.mean(zc * zc, axis=-1, keepdims=True)
    return zc * lax.rsqrt(var + EPS) * g + b


def _dot(a, b):
    return jnp.dot(a, b, preferred_element_type=F32)


def _dot_nt(a, b):
    return lax.dot_general(a, b, (((1,), (1,)), ((), ())), preferred_element_type=F32)


def _mod_kernel(c_ref, w_ref, b_ref, o_ref):
    s = jax.nn.silu(c_ref[...])
    o_ref[...] = jnp.dot(s, w_ref[...], preferred_element_type=F32,
                         precision=lax.Precision.HIGHEST) + b_ref[...]


def _modulation(cc, ada_w, ada_b):
    rows = cc.shape[0]
    tn = 1536
    out = pl.pallas_call(
        _mod_kernel,
        grid=(DEPTH, 6 * D_MODEL // tn),
        in_specs=[
            pl.BlockSpec((rows, D_MODEL), lambda l, j: (0, 0)),
            pl.BlockSpec((None, D_MODEL, tn), lambda l, j: (l, 0, j)),
            pl.BlockSpec((None, 1, tn), lambda l, j: (l, 0, j)),
        ],
        out_specs=pl.BlockSpec((None, rows, tn), lambda l, j: (l, 0, j)),
        out_shape=jax.ShapeDtypeStruct((DEPTH, rows, 6 * D_MODEL), F32),
        compiler_params=_params("parallel", "parallel"),
        name="adaln_mod",
    )(cc, ada_w, ada_b.reshape(DEPTH, 1, 6 * D_MODEL))
    return out.reshape(DEPTH, rows, 1, 6 * D_MODEL)


def _mod_spec(chunk, per_batch, ctx_row):
    if per_batch:
        return pl.BlockSpec((None, 1, D_MODEL), lambda b, *_: (b, 0, chunk))
    return pl.BlockSpec((None, 1, D_MODEL), lambda b, *_: (ctx_row, 0, chunk))


class Split(NamedTuple):
    start: int
    width: int
    dtype: object
    scale: float
    norm: bool
    rope: bool


def _group_rms_norm(p, gain, gmat):
    ss = p * p
    hi = ss.astype(BF16)
    lo = (ss - hi.astype(F32)).astype(BF16)
    blocks = []
    for j in range(p.shape[1] // MXU_DIM):
        sl = slice(j * MXU_DIM, (j + 1) * MXU_DIM)
        blocks.append(_dot(hi[:, sl], gmat) + _dot(lo[:, sl], gmat))
    ms = blocks[0] if len(blocks) == 1 else jnp.concatenate(blocks, axis=1)
    return p * lax.rsqrt(ms + EPS) * gain


def _rope(p, cos, sin):
    tm = p.shape[0]
    lane = lax.broadcasted_iota(jnp.int32, (tm, LANES), 1)
    first = (lane & 16) == 0
    outs = []
    for j in range(p.shape[1] // LANES):
        xb = p[:, j * LANES:(j + 1) * LANES]
        partner = jnp.where(first, pltpu.roll(xb, LANES - 16, 1), pltpu.roll(xb, 16, 1))
        outs.append(xb * cos + partner * sin)
    return jnp.concatenate(outs, axis=1)


def _inproj_kernel(x_ref, sc_ref, sh_ref, w_ref, wvt_ref, *rest, splits, use_norm, use_rope):
    idx = 0
    if use_norm:
        gain_ref, gmat_ref = rest[0], rest[1]
        idx = 2
    if use_rope:
        cos_ref, sin_ref = rest[idx], rest[idx + 1]
        idx += 2
    out_refs = rest[idx:-1]
    vt_ref = rest[-1]
    sub = min(ROW_TILE, x_ref.shape[0])
    n_sub = x_ref.shape[0] // sub

    def project(r):
        rs = slice(r * sub, (r + 1) * sub)
        h = (x_ref[rs, :] * (1.0 + sc_ref[...]) + sh_ref[...]).astype(BF16)
        vt_ref[:, rs] = _dot_nt(wvt_ref[...], h).astype(vt_ref.dtype)
        return [_dot(h, w_ref[:, sp.start:sp.start + sp.width]) for sp in splits]

    def finish(r, ps):
        rs = slice(r * sub, (r + 1) * sub)
        gain_off = 0
        for sp, p, o_ref in zip(splits, ps, out_refs):
            if sp.norm:
                p = _group_rms_norm(p, gain_ref[:, gain_off:gain_off + sp.width], gmat_ref[...])
                gain_off += sp.width
            if sp.rope and use_rope:
                p = _rope(p, cos_ref[rs, :], sin_ref[rs, :])
            if sp.scale != 1.0:
                p = p * sp.scale
            o_ref[rs, :] = p.astype(o_ref.dtype)

    ps_next = project(0)
    for r in range(n_sub):
        ps = ps_next
        if r + 1 < n_sub:
            ps_next = project(r + 1)
        finish(r, ps)


def _inproj(x, mod, w, wvt, splits, *, tm, per_batch, ctx_row, gain=None, gmat=None, rope=None):
    bsz, rows, _ = x.shape
    n_cols = w.shape[1]
    v_width = wvt.shape[0]
    use_norm = gain is not None
    use_rope = rope is not None
    in_specs = [
        pl.BlockSpec((None, tm, D_MODEL), lambda b, i: (b, i, 0)),
        _mod_spec(1, per_batch, ctx_row),
        _mod_spec(0, per_batch, ctx_row),
        pl.BlockSpec((D_MODEL, n_cols), lambda b, i: (0, 0)),
        pl.BlockSpec((v_width, D_MODEL), lambda b, i: (0, 0)),
    ]
    args = [x, mod, mod, w, wvt]
    if use_norm:
        in_specs += [pl.BlockSpec(gain.shape, lambda b, i: (0, 0)),
                     pl.BlockSpec(gmat.shape, lambda b, i: (0, 0))]
        args += [gain, gmat]
    if use_rope:
        in_specs += [pl.BlockSpec((tm, LANES), lambda b, i: (i, 0))] * 2
        args += [rope[0], rope[1]]
    out_specs = [pl.BlockSpec((None, tm, sp.width), lambda b, i: (b, i, 0)) for sp in splits]
    out_shape = [jax.ShapeDtypeStruct((bsz, rows, sp.width), sp.dtype) for sp in splits]
    out_specs.append(pl.BlockSpec((None, v_width, tm), lambda b, i: (b, 0, i)))
    out_shape.append(jax.ShapeDtypeStruct((bsz, v_width, rows), BF16))
    return pl.pallas_call(
        functools.partial(_inproj_kernel, splits=tuple(splits), use_norm=use_norm, use_rope=use_rope),
        grid=(bsz, rows // tm),
        in_specs=in_specs,
        out_specs=out_specs,
        out_shape=out_shape,
        compiler_params=_params("parallel", "parallel"),
        name="inproj",
    )(*args)


def _half_masks(shape, dtype):
    lane = lax.broadcasted_iota(jnp.int32, shape, 1)
    lo = (lane < HEAD_DIM).astype(dtype)
    return lo, (1 - lo).astype(dtype)


def _softmax_pv_step(state, s, vt):
    cm = jnp.max(s, axis=0, keepdims=True)
    if state is None:
        return cm, _dot(vt, jnp.exp2(s - cm).astype(BF16))
    m, o = state
    m_new = jnp.maximum(m, cm)
    return m_new, jnp.exp2(m - m_new) * o + _dot(vt, jnp.exp2(s - m_new).astype(BF16))


def _attend(groups, k_refs, vt_refs, finish):
    chunks = [(k_ref, vt_ref, slice(c * KEY_CHUNK, (c + 1) * KEY_CHUNK))
              for k_ref, vt_ref in zip(k_refs, vt_refs) for c in range(k_ref.shape[0] // KEY_CHUNK)]
    items = [(g, j) for g in range(len(groups)) for j in range(len(chunks))]

    def scores(item):
        g, j = item
        k_ref, _, ks = chunks[j]
        kc = k_ref[ks, :]
        return [_dot_nt(kc, qm) for qm, _ in groups[g]]

    ones = jnp.ones((BF16_SUBLANES, KEY_CHUNK), BF16)
    pending = [scores(item) for item in items[:QK_LOOKAHEAD]]
    for t, (g, j) in enumerate(items):
        s_cur = pending.pop(0)
        if t + QK_LOOKAHEAD < len(items):
            pending.append(scores(items[t + QK_LOOKAHEAD]))
        if j == 0:
            state = [None] * len(groups[g])
        _, vt_ref, ks = chunks[j]
        state = [_softmax_pv_step(st, s, jnp.concatenate([vt_ref[v_rows, ks], ones], axis=0))
                 for st, s, (_, v_rows) in zip(state, s_cur, groups[g])]
        if j == len(chunks) - 1:
            finish(g, [(o[:-BF16_SUBLANES], o[-BF16_SUBLANES:-BF16_SUBLANES + 1]) for _, o in state])


def _attn_gqa_kernel(q_ref, *refs, n_parts):
    k_refs = refs[:n_parts]
    vt_refs = refs[n_parts:2 * n_parts]
    o_ref = refs[2 * n_parts]
    masks = _half_masks((Q_TILE, LANES), BF16)
    groups, places = [], []
    for r in range(q_ref.shape[0] // Q_TILE):
        for pr in range(q_ref.shape[1] // LANES):
            place = (slice(r * Q_TILE, (r + 1) * Q_TILE), slice(pr * LANES, (pr + 1) * LANES))
            qp = q_ref[place]
            groups.append([(qp * mask, slice(half * HEAD_DIM, (half + 1) * HEAD_DIM))
                           for half, mask in enumerate(masks)])
            places.append(place)

    def finish(g, pair):
        o_pair = jnp.concatenate([o / l for o, l in pair], axis=0)
        o_ref[places[g]] = o_pair.T.astype(o_ref.dtype)

    _attend(groups, k_refs, vt_refs, finish)


def _attn_gqa(q, ks, vts, *, tq):
    bsz, rows, _ = q.shape
    n_parts = len(ks)
    qw = A_WIDTH // 2
    k_specs = [pl.BlockSpec((None, k.shape[1], LANES), lambda b, p, i: (b, 0, p)) for k in ks]
    vt_specs = [pl.BlockSpec((None, LANES, vt.shape[2]), lambda b, p, i: (b, p, 0)) for vt in vts]
    return pl.pallas_call(
        functools.partial(_attn_gqa_kernel, n_parts=n_parts),
        grid=(bsz, 2, rows // tq),
        in_specs=[pl.BlockSpec((None, tq, qw), lambda b, p, i: (b, i, p))] + k_specs + vt_specs,
        out_specs=pl.BlockSpec((None, tq, qw), lambda b, p, i: (b, i, p)),
        out_shape=jax.ShapeDtypeStruct((bsz, rows, A_WIDTH), BF16),
        compiler_params=_params("parallel", "parallel", "parallel"),
        name="attn_gqa",
    )(q, *ks, *vts)


def _attn_diff_kernel(lam_ref, gain_ref, q_ref, *refs, n_parts, lam_init):
    k_refs = refs[:n_parts]
    vt_refs = refs[n_parts:2 * n_parts]
    o_ref = refs[2 * n_parts]
    lv = lam_ref[...]
    lam = (jnp.exp(jnp.sum(lv[0:1] * lv[1:2], axis=1, keepdims=True))
           - jnp.exp(jnp.sum(lv[2:3] * lv[3:4], axis=1, keepdims=True)) + lam_init)
    masks = _half_masks((Q_TILE, LANES), BF16)
    rows = slice(0, C_V_DIM)
    groups = []
    for r in range(q_ref.shape[0] // Q_TILE):
        q = q_ref[r * Q_TILE:(r + 1) * Q_TILE, :]
        groups.append([(q * mask, rows) for mask in masks])

    def finish(g, comps):
        (o0, l0), (o1, l1) = comps
        o = o0 * (1.0 / l0) - o1 * (lam / l1)
        ms = jnp.mean(o * o, axis=0, keepdims=True)
        o = (o * lax.rsqrt(ms + EPS)).T
        o_ref[g * Q_TILE:(g + 1) * Q_TILE, :] = (o * gain_ref[...] * (1.0 - lam_init)).astype(o_ref.dtype)

    _attend(groups, k_refs, vt_refs, finish)


def _attn_diff(lam_vecs, gain, q, ks, vts, *, tq, lam_init):
    bsz, rows, _ = q.shape
    n_parts = len(ks)
    heads = C_WIDTH // C_V_DIM
    k_specs = [pl.BlockSpec((None, k.shape[1], LANES), lambda b, h, i: (b, 0, h)) for k in ks]
    vt_specs = [pl.BlockSpec((None, C_V_DIM, vt.shape[2]), lambda b, h, i: (b, h, 0)) for vt in vts]
    return pl.pallas_call(
        functools.partial(_attn_diff_kernel, n_parts=n_parts, lam_init=lam_init),
        grid=(bsz, heads, rows // tq),
        in_specs=[pl.BlockSpec(lam_vecs.shape, lambda b, h, i: (0, 0)),
                  pl.BlockSpec(gain.shape, lambda b, h, i: (0, 0)),
                  pl.BlockSpec((None, tq, LANES), lambda b, h, i: (b, i, h))] + k_specs + vt_specs,
        out_specs=pl.BlockSpec((None, tq, LANES), lambda b, h, i: (b, i, h)),
        out_shape=jax.ShapeDtypeStruct((bsz, rows, C_WIDTH), BF16),
        compiler_params=_params("parallel", "parallel", "parallel"),
        name="attn_diff",
    )(lam_vecs, gain, q, *ks, *vts)


def _pool_kernel(u_ref, w_ref, ps_ref, o_ref, sh_ref, *, chunk):
    t_len = u_ref.shape[0]
    padded = t_len + 2 * POOL_PAD
    zeros = jnp.zeros((POOL_PAD, B_WIDTH), F32)
    sh_ref[0, 0:POOL_PAD, :] = zeros
    sh_ref[0, POOL_PAD + t_len:padded, :] = zeros
    sh_ref[0, POOL_PAD:POOL_PAD + t_len, :] = u_ref[...]
    shifted_len = padded - SUBLANES
    for r in range(1, SUBLANES):
        for c0 in range(0, shifted_len, chunk):
            n = min(chunk, shifted_len - c0)
            sh_ref[r, c0:c0 + n, :] = sh_ref[0, c0 + r:c0 + r + n, :]
    grp = lax.broadcasted_iota(jnp.int32, (chunk, B_WIDTH), 1) // B_GROUP_DIM
    for c in range(t_len // chunk):
        r0 = c * chunk

        def win(j):
            off = POOL_PAD + j
            lo = r0 + off // SUBLANES * SUBLANES
            return sh_ref[off % SUBLANES, lo:lo + chunk, :]

        tok = win(0)
        sums = []
        acc = None
        for w in POOL_WINDOWS:
            lo, hi = -(w // 2), w - 1 - w // 2
            if acc is None:
                acc = functools.reduce(jnp.add, [win(j) for j in range(lo, hi + 1)])
            else:
                plo, phi = -(prev_w // 2), prev_w - 1 - prev_w // 2
                acc = functools.reduce(jnp.add, [acc] + [win(j) for j in range(lo, plo)]
                                       + [win(j) for j in range(phi + 1, hi + 1)])
            prev_w = w
            sums.append(acc)
        t = lax.broadcasted_iota(jnp.int32, (chunk, B_WIDTH), 0) + r0
        half = jnp.where(grp == 0, 1, jnp.where(grp == 1, 2, jnp.where(grp == 2, 4, 8)))
        tail = half - 1
        cnt = (jnp.minimum(t + tail + 1, t_len) - jnp.maximum(t - half, 0)).astype(F32)
        s = jnp.where(grp == 0, sums[0], jnp.where(grp == 1, sums[1], jnp.where(grp == 2, sums[2], sums[3])))
        pooled = s / cnt - tok
        mixed = _dot(pooled.astype(BF16), w_ref[...]) * ps_ref[...]
        o_ref[r0:r0 + chunk, :] = mixed.astype(o_ref.dtype)


def _pool(u, w_bd, pool_scale):
    bsz, t_len, _ = u.shape
    return pl.pallas_call(
        functools.partial(_pool_kernel, chunk=min(t_len, 256)),
        grid=(bsz,),
        in_specs=[pl.BlockSpec((None, t_len, B_WIDTH), lambda b: (b, 0, 0)),
                  pl.BlockSpec(w_bd.shape, lambda b: (0, 0)),
                  pl.BlockSpec(pool_scale.shape, lambda b: (0, 0))],
        out_specs=pl.BlockSpec((None, t_len, B_WIDTH), lambda b: (b, 0, 0)),
        out_shape=jax.ShapeDtypeStruct((bsz, t_len, B_WIDTH), BF16),
        scratch_shapes=[pltpu.VMEM((SUBLANES, t_len + 2 * POOL_PAD, B_WIDTH), F32)],
        compiler_params=_params("parallel"),
        name="pool_mixer",
    )(u, w_bd, pool_scale)


def _conv_kernel(u_ref, w_ref, b_ref, g_ref, beta_ref, o_ref, sh_ref, *, chunk):
    t_len = u_ref.shape[0]
    padded = t_len + 2 * CONV_PAD
    zeros = jnp.zeros((CONV_PAD, D_CH), F32)
    sh_ref[0, 0:CONV_PAD, :] = zeros
    sh_ref[0, CONV_PAD + t_len:padded, :] = zeros
    for c in range(t_len // chunk):
        rs = slice(c * chunk, (c + 1) * chunk)
        sh_ref[0, CONV_PAD + c * chunk:CONV_PAD + (c + 1) * chunk, :] = (
            u_ref[rs, 0:D_CH] * jax.nn.sigmoid(u_ref[rs, D_CH:2 * D_CH]))
    shifted_len = padded - SUBLANES
    for r in range(1, SUBLANES):
        for c0 in range(0, shifted_len, chunk):
            n = min(chunk, shifted_len - c0)
            sh_ref[r, c0:c0 + n, :] = sh_ref[0, c0 + r:c0 + r + n, :]
    first = CONV_PAD - CONV_WIDTH // 2
    for c in range(t_len // chunk):
        acc = None
        for j in range(CONV_WIDTH):
            off = first + j
            lo = c * chunk + off // SUBLANES * SUBLANES
            term = sh_ref[off % SUBLANES, lo:lo + chunk, :] * w_ref[j:j + 1, :]
            acc = term if acc is None else acc + term
        z = _layer_norm(acc + b_ref[...], g_ref[...], beta_ref[...])
        o_ref[c * chunk:(c + 1) * chunk, :] = jax.nn.silu(z).astype(o_ref.dtype)


def _conformer_conv(u, conv_w, conv_b, ln_g, ln_b):
    bsz, t_len, _ = u.shape
    vec = pl.BlockSpec((1, D_CH), lambda b: (0, 0))
    return pl.pallas_call(
        functools.partial(_conv_kernel, chunk=128),
        grid=(bsz,),
        in_specs=[pl.BlockSpec((None, t_len, 2 * D_CH), lambda b: (b, 0, 0)),
                  pl.BlockSpec(conv_w.shape, lambda b: (0, 0)), vec, vec, vec],
        out_specs=pl.BlockSpec((None, t_len, D_CH), lambda b: (b, 0, 0)),
        out_shape=jax.ShapeDtypeStruct((bsz, t_len, D_CH), BF16),
        scratch_shapes=[pltpu.VMEM((SUBLANES, t_len + 2 * CONV_PAD, D_CH), F32)],
        compiler_params=_params("parallel"),
        name="conformer_conv",
    )(u, conv_w, conv_b, ln_g, ln_b)


def _outproj_kernel(a_ref, m_ref, w_ref, x_ref, gate_ref, g_ref, b_ref, o_ref):
    wa = a_ref.shape[1]
    sub = min(ROW_TILE, x_ref.shape[0])
    n_sub = x_ref.shape[0] // sub

    def project(r):
        rs = slice(r * sub, (r + 1) * sub)
        return _dot(a_ref[rs, :], w_ref[0:wa, :]) + _dot(m_ref[rs, :], w_ref[wa:, :])

    y_next = project(0)
    for r in range(n_sub):
        y = y_next
        if r + 1 < n_sub:
            y_next = project(r + 1)
        rs = slice(r * sub, (r + 1) * sub)
        z = ALPHA * x_ref[rs, :] + gate_ref[...] * y
        o_ref[rs, :] = _layer_norm(z, g_ref[...], b_ref[...])


def _outproj(attn, mix, w_out, x, mod, ln_g, ln_b, *, tm, per_batch, ctx_row):
    bsz, rows, _ = x.shape
    vec = pl.BlockSpec((1, D_MODEL), lambda b, i: (0, 0))
    return pl.pallas_call(
        _outproj_kernel,
        grid=(bsz, rows // tm),
        in_specs=[pl.BlockSpec((None, tm, attn.shape[2]), lambda b, i: (b, i, 0)),
                  pl.BlockSpec((None, tm, mix.shape[2]), lambda b, i: (b, i, 0)),
                  pl.BlockSpec(w_out.shape, lambda b, i: (0, 0)),
                  pl.BlockSpec((None, tm, D_MODEL), lambda b, i: (b, i, 0)),
                  _mod_spec(2, per_batch, ctx_row), vec, vec],
        out_specs=pl.BlockSpec((None, tm, D_MODEL), lambda b, i: (b, i, 0)),
        out_shape=jax.ShapeDtypeStruct(x.shape, F32),
        compiler_params=_params("parallel", "parallel"),
        name="outproj_ln",
    )(attn, mix, w_out, x, mod, ln_g, ln_b)


def _ffn_kernel(x_ref, sc_ref, sh_ref, gate_ref, wi_ref, wo_ref, g_ref, b_ref, o_ref, acc_ref):
    h = (x_ref[...] * (1.0 + sc_ref[...]) + sh_ref[...]).astype(BF16)
    n_chunks = FFN_HIDDEN // FFN_CHUNK

    def up(c):
        lo = c * FFN_CHUNK
        return (_dot(h, wi_ref[:, lo:lo + FFN_CHUNK]),
                _dot(h, wi_ref[:, FFN_HIDDEN + lo:FFN_HIDDEN + lo + FFN_CHUNK]))

    pending = [up(c) for c in range(FFN_LOOKAHEAD)]
    for c in range(n_chunks):
        a, g = pending.pop(0)
        if c + FFN_LOOKAHEAD < n_chunks:
            pending.append(up(c + FFN_LOOKAHEAD))
        y = _dot((jax.nn.silu(g) * a).astype(BF16), wo_ref[c * FFN_CHUNK:(c + 1) * FFN_CHUNK, :])
        if c == 0:
            acc_ref[...] = y
        else:
            acc_ref[...] += y
    z = ALPHA * x_ref[...] + gate_ref[...] * acc_ref[...]
    o_ref[...] = _layer_norm(z, g_ref[...], b_ref[...])


def _ffn(x, mod, w_in, w_out, layer, ln_g, ln_b, *, tm, per_batch, ctx_row):
    bsz, rows, _ = x.shape
    vec = pl.BlockSpec((1, D_MODEL), lambda b, i: (0, 0))
    resident = dict(pipeline_mode=pl.Buffered(1))
    return pl.pallas_call(
        _ffn_kernel,
        grid=(bsz, rows // tm),
        in_specs=[pl.BlockSpec((None, tm, D_MODEL), lambda b, i: (b, i, 0)),
                  _mod_spec(4, per_batch, ctx_row),
                  _mod_spec(3, per_batch, ctx_row),
                  _mod_spec(5, per_batch, ctx_row),
                  pl.BlockSpec((None,) + w_in.shape[1:], lambda b, i: (layer, 0, 0), **resident),
                  pl.BlockSpec((None,) + w_out.shape[1:], lambda b, i: (layer, 0, 0), **resident),
                  vec, vec],
        out_specs=pl.BlockSpec((None, tm, D_MODEL), lambda b, i: (b, i, 0)),
        out_shape=jax.ShapeDtypeStruct(x.shape, F32),
        scratch_shapes=[pltpu.VMEM((tm, D_MODEL), F32)],
        compiler_params=_params("parallel", "parallel"),
        name="ffn_ln",
    )(x, mod, mod, mod, w_in, w_out, ln_g, ln_b)


def _rope_tables(n_tokens):
    rows = n_tokens // GRID_W
    row = jnp.repeat(jnp.arange(rows, dtype=F32), GRID_W)
    col = jnp.tile(jnp.arange(GRID_W, dtype=F32), rows)
    axis_dim = HEAD_DIM // 2
    freqs = ROPE_THETA ** (-jnp.arange(0, axis_dim, 2, dtype=F32) / axis_dim)
    ang_r, ang_c = row[:, None] * freqs, col[:, None] * freqs
    cos = jnp.concatenate([jnp.cos(ang_r)] * 2 + [jnp.cos(ang_c)] * 2, axis=-1)
    sin = jnp.concatenate([-jnp.sin(ang_r), jnp.sin(ang_r), -jnp.sin(ang_c), jnp.sin(ang_c)], axis=-1)
    return jnp.tile(cos, (1, 2)), jnp.tile(sin, (1, 2))


def _pair_heads_cols(w):
    return w.reshape(w.shape[0], 2, 2, 3, HEAD_DIM).transpose(0, 1, 3, 2, 4).reshape(w.shape[0], A_WIDTH)


def _pair_heads_rows(w):
    return w.reshape(2, 2, 3, HEAD_DIM, w.shape[1]).transpose(0, 2, 1, 3, 4).reshape(A_WIDTH, w.shape[1])


def _block_diag(w_pool):
    out = jnp.zeros((B_WIDTH, B_WIDTH), w_pool.dtype)
    for g in range(B_GROUPS):
        sl = slice(g * B_GROUP_DIM, (g + 1) * B_GROUP_DIM)
        out = out.at[sl, sl].set(w_pool[g])
    return out


def kernel(x, c, ctx, c_ctx, ab_w_in, ab_q_gain, ab_k_gain, ab_w_pool, ab_pool_scale, ab_w_out,
           cd_w_in, cd_lambda_q1, cd_lambda_k1, cd_lambda_q2, cd_lambda_k2, cd_subln_gain,
           cd_conv_w, cd_conv_b, cd_conv_ln_g, cd_conv_ln_b, cd_w_out,
           ada_w, ada_b, ln1_g, ln1_b, ln2_g, ln2_b, ffn_w_in, ffn_w_out):
    bsz, t_len, _ = x.shape
    n_ctx = ctx.shape[1]
    ctx_row = bsz
    mod_rows = -(-(bsz + 1) // 8) * 8
    cc = jnp.zeros((mod_rows, D_MODEL), F32).at[:bsz].set(c).at[bsz].set(c_ctx)
    mods = _modulation(cc, ada_w, ada_b)
    rope = _rope_tables(t_len)
    row = lambda v: v.reshape(1, -1)
    head_of = jnp.arange(MXU_DIM) // HEAD_DIM
    gmat = jnp.where(head_of[:, None] == head_of[None, :], 1.0 / HEAD_DIM, 0.0).astype(BF16)
    w_ffn_in, w_ffn_out = ffn_w_in.astype(BF16), ffn_w_out.astype(BF16)

    l, i = 0, 0
    v0 = A_WIDTH + A_KV_WIDTH
    w_in = jnp.concatenate([_pair_heads_cols(ab_w_in[i][:, :A_WIDTH]), ab_w_in[i][:, A_WIDTH:]], axis=1).astype(BF16)
    w_vt = ab_w_in[i][:, v0:v0 + A_KV_WIDTH].T.astype(BF16)
    w_out = jnp.concatenate([_pair_heads_rows(ab_w_out[i][:A_WIDTH]), ab_w_out[i][A_WIDTH:]], axis=0).astype(BF16)
    gain = jnp.concatenate([jnp.tile(ab_q_gain[i], A_WIDTH // HEAD_DIM),
                            jnp.tile(ab_k_gain[i], A_KV_WIDTH // HEAD_DIM)]).reshape(1, -1)
    splits = [Split(0, A_WIDTH, BF16, Q_SCALE, True, True),
              Split(A_WIDTH, A_KV_WIDTH, BF16, 1.0, True, True),
              Split(v0 + A_KV_WIDTH, B_WIDTH, F32, 1.0, False, False)]
    w_bd = _block_diag(ab_w_pool[i]).astype(BF16)
    pool_scale = row(ab_pool_scale[i])
    g1, b1, g2, b2 = row(ln1_g[l]), row(ln1_b[l]), row(ln2_g[l]), row(ln2_b[l])

    q, k, u, vt = _inproj(x, mods[l], w_in, w_vt, splits, tm=PROJ_TM, per_batch=True, ctx_row=ctx_row,
                          gain=gain, gmat=gmat, rope=rope)
    qc, kc, uc, vtc = _inproj(ctx, mods[l], w_in, w_vt, splits, tm=n_ctx, per_batch=False, ctx_row=ctx_row,
                              gain=gain, gmat=gmat)
    o = _attn_gqa(q, [kc, k], [vtc, vt], tq=ATTN_TQ)
    oc = _attn_gqa(qc, [kc], [vtc], tq=n_ctx)
    mix = _pool(u, w_bd, pool_scale)
    mixc = _pool(uc, w_bd, pool_scale)
    x = _outproj(o, mix, w_out, x, mods[l], g1, b1, tm=PROJ_TM, per_batch=True, ctx_row=ctx_row)
    xc = _outproj(oc, mixc, w_out, ctx, mods[l], g1, b1, tm=n_ctx, per_batch=False, ctx_row=ctx_row)
    x = _ffn(x, mods[l], w_ffn_in, w_ffn_out, l, g2, b2, tm=FFN_TM, per_batch=True, ctx_row=ctx_row)
    xc = _ffn(xc.reshape(1, bsz * n_ctx, D_MODEL), mods[l], w_ffn_in, w_ffn_out, l, g2, b2,
              tm=FFN_TM, per_batch=False, ctx_row=ctx_row).reshape(bsz, n_ctx, D_MODEL)

    l, i = 1, 0
    lam_init = 0.8 - 0.6 * math.exp(-0.3 * l)
    w_in = cd_w_in[i].astype(BF16)
    w_vt = cd_w_in[i][:, 2 * C_WIDTH:3 * C_WIDTH].T.astype(BF16)
    w_out = cd_w_out[i].astype(BF16)
    splits = [Split(0, C_WIDTH, BF16, Q_SCALE, False, True),
              Split(C_WIDTH, C_WIDTH, BF16, 1.0, False, True),
              Split(3 * C_WIDTH, 2 * D_CH, F32, 1.0, False, False)]
    ctx_splits = [Split(C_WIDTH, C_WIDTH, BF16, 1.0, False, False)]
    lam_vecs = jnp.stack([cd_lambda_q1[i], cd_lambda_k1[i], cd_lambda_q2[i], cd_lambda_k2[i]])
    sub_gain = row(cd_subln_gain[i])
    g1, b1, g2, b2 = row(ln1_g[l]), row(ln1_b[l]), row(ln2_g[l]), row(ln2_b[l])

    q, k, u, vt = _inproj(x, mods[l], w_in, w_vt, splits, tm=PROJ_TM, per_batch=True, ctx_row=ctx_row, rope=rope)
    kc, vtc = _inproj(xc, mods[l], w_in, w_vt, ctx_splits, tm=n_ctx, per_batch=False, ctx_row=ctx_row)
    o = _attn_diff(lam_vecs, sub_gain, q, [kc, k], [vtc, vt], tq=ATTN_TQ, lam_init=lam_init)
    conv = _conformer_conv(u, cd_conv_w[i], row(cd_conv_b[i]), row(cd_conv_ln_g[i]), row(cd_conv_ln_b[i]))
    x = _outproj(o, conv, w_out, x, mods[l], g1, b1, tm=PROJ_TM, per_batch=True, ctx_row=ctx_row)
    x = _ffn(x, mods[l], w_ffn_in, w_ffn_out, l, g2, b2, tm=FFN_TM, per_batch=True, ctx_row=ctx_row)
    return x
```

```python
import functools
import math
from typing import NamedTuple

import jax
import jax.numpy as jnp
from jax import lax
from jax.experimental import pallas as pl
from jax.experimental.pallas import tpu as pltpu

F32 = jnp.float32
BF16 = jnp.bfloat16

D_MODEL = 1024
DEPTH = 2
GRID_W = 64
HEAD_DIM = 64
ROPE_THETA = 10000.0
EPS = 1e-6

B_WIDTH = D_MODEL // 4
B_GROUPS = 4
B_GROUP_DIM = B_WIDTH // B_GROUPS
POOL_WINDOWS = (2, 4, 8, 16)
A_WIDTH = D_MODEL - B_WIDTH
A_KV_WIDTH = 4 * HEAD_DIM
D_CH = D_MODEL // 4
C_WIDTH = D_MODEL - D_CH
C_V_DIM = 2 * HEAD_DIM
CONV_WIDTH = 31
FFN_HIDDEN = -(-8 * D_MODEL // (3 * 256)) * 256
ALPHA = (2.0 * DEPTH) ** 0.25

LANES = 128
SUBLANES = 8
BF16_SUBLANES = 16
Q_SCALE = HEAD_DIM ** -0.5 * math.log2(math.e)
MXU_DIM = 256
VMEM_LIMIT = 48 * 1024 * 1024

POOL_PAD = 8
CONV_PAD = 16
FFN_CHUNK = 256
FFN_LOOKAHEAD = 1
KEY_CHUNK = MXU_DIM
QK_LOOKAHEAD = 2
ROW_TILE = MXU_DIM
PROJ_TM = 1024
FFN_TM = 512
Q_TILE = MXU_DIM
GQA_TQ = 1024
DIFF_TQ = 2048


def _params(*semantics):
    return pltpu.CompilerParams(dimension_semantics=semantics, vmem_limit_bytes=VMEM_LIMIT)


def _layer_norm(z, g, b):
    mu = jnp.mean(z, axis=-1, keepdims=True)
    zc = z - mu
    var = jnp.mean(zc * zc, axis=-1, keepdims=True)
    return zc * lax.rsqrt(var + EPS) * g + b


def _dot(a, b):
    return jnp.dot(a, b, preferred_element_type=F32)


def _dot_nt(a, b):
    return lax.dot_general(a, b, (((1,), (1,)), ((), ())), preferred_element_type=F32)


def _mod_kernel(c_ref, w_ref, b_ref, o_ref):
    s = jax.nn.silu(c_ref[...])
    o_ref[...] = jnp.dot(s, w_ref[...], preferred_element_type=F32,
                         precision=lax.Precision.HIGHEST) + b_ref[...]


def _modulation(cc, ada_w, ada_b):
    rows = cc.shape[0]
    tn = 1536
    out = pl.pallas_call(
        _mod_kernel,
        grid=(DEPTH, 6 * D_MODEL // tn),
        in_specs=[
            pl.BlockSpec((rows, D_MODEL), lambda l, j: (0, 0)),
            pl.BlockSpec((None, D_MODEL, tn), lambda l, j: (l, 0, j)),
            pl.BlockSpec((None, 1, tn), lambda l, j: (l, 0, j)),
        ],
        out_specs=pl.BlockSpec((None, rows, tn), lambda l, j: (l, 0, j)),
        out_shape=jax.ShapeDtypeStruct((DEPTH, rows, 6 * D_MODEL), F32),
        compiler_params=_params("parallel", "parallel"),
        name="adaln_mod",
    )(cc, ada_w, ada_b.reshape(DEPTH, 1, 6 * D_MODEL))
    return out.reshape(DEPTH, rows, 1, 6 * D_MODEL)


def _mod_spec(chunk, per_batch, ctx_row):
    if per_batch:
        return pl.BlockSpec((None, 1, D_MODEL), lambda b, *_: (b, 0, chunk))
    return pl.BlockSpec((None, 1, D_MODEL), lambda b, *_: (ctx_row, 0, chunk))


class Split(NamedTuple):
    start: int
    width: int
    dtype: object
    scale: float
    norm: bool
    rope: bool


def _group_rms_norm(p, gain, gmat):
    ss = p * p
    hi = ss.astype(BF16)
    lo = (ss - hi.astype(F32)).astype(BF16)
    blocks = []
    for j in range(p.shape[1] // MXU_DIM):
        sl = slice(j * MXU_DIM, (j + 1) * MXU_DIM)
        blocks.append(_dot(hi[:, sl], gmat) + _dot(lo[:, sl], gmat))
    ms = blocks[0] if len(blocks) == 1 else jnp.concatenate(blocks, axis=1)
    return p * lax.rsqrt(ms + EPS) * gain


def _rope(p, cos, sin):
    tm = p.shape[0]
    lane = lax.broadcasted_iota(jnp.int32, (tm, LANES), 1)
    first = (lane & 16) == 0
    outs = []
    for j in range(p.shape[1] // LANES):
        xb = p[:, j * LANES:(j + 1) * LANES]
        partner = jnp.where(first, pltpu.roll(xb, LANES - 16, 1), pltpu.roll(xb, 16, 1))
        outs.append(xb * cos + partner * sin)
    return jnp.concatenate(outs, axis=1)


def _inproj_kernel(x_ref, sc_ref, sh_ref, w_ref, wvt_ref, *rest, splits, use_norm, use_rope):
    idx = 0
    if use_norm:
        gain_ref, gmat_ref = rest[0], rest[1]
        idx = 2
    if use_rope:
        cos_ref, sin_ref = rest[idx], rest[idx + 1]
        idx += 2
    out_refs = rest[idx:-1]
    vt_ref = rest[-1]
    sub = min(ROW_TILE, x_ref.shape[0])
    n_sub = x_ref.shape[0] // sub

    def project(r):
        rs = slice(r * sub, (r + 1) * sub)
        h = (x_ref[rs, :] * (1.0 + sc_ref[...]) + sh_ref[...]).astype(BF16)
        vt_ref[:, rs] = _dot_nt(wvt_ref[...], h).astype(vt_ref.dtype)
        return [_dot(h, w_ref[:, sp.start:sp.start + sp.width]) for sp in splits]

    def finish(r, ps):
        rs = slice(r * sub, (r + 1) * sub)
        gain_off = 0
        for sp, p, o_ref in zip(splits, ps, out_refs):
            if sp.norm:
                p = _group_rms_norm(p, gain_ref[:, gain_off:gain_off + sp.width], gmat_ref[...])
                gain_off += sp.width
            if sp.rope and use_rope:
                p = _rope(p, cos_ref[rs, :], sin_ref[rs, :])
            if sp.scale != 1.0:
                p = p * sp.scale
            o_ref[rs, :] = p.astype(o_ref.dtype)

    ps_next = project(0)
    for r in range(n_sub):
        ps = ps_next
        if r + 1 < n_sub:
            ps_next = project(r + 1)
        finish(r, ps)


def _inproj(x, mod, w, wvt, splits, *, tm, per_batch, ctx_row, gain=None, gmat=None, rope=None):
    bsz, rows, _ = x.shape
    n_cols = w.shape[1]
    v_width = wvt.shape[0]
    use_norm = gain is not None
    use_rope = rope is not None
    in_specs = [
        pl.BlockSpec((None, tm, D_MODEL), lambda b, i: (b, i, 0)),
        _mod_spec(1, per_batch, ctx_row),
        _mod_spec(0, per_batch, ctx_row),
        pl.BlockSpec((D_MODEL, n_cols), lambda b, i: (0, 0)),
        pl.BlockSpec((v_width, D_MODEL), lambda b, i: (0, 0)),
    ]
    args = [x, mod, mod, w, wvt]
    if use_norm:
        in_specs += [pl.BlockSpec(gain.shape, lambda b, i: (0, 0)),
                     pl.BlockSpec(gmat.shape, lambda b, i: (0, 0))]
        args += [gain, gmat]
    if use_rope:
        in_specs += [pl.BlockSpec((tm, LANES), lambda b, i: (i, 0))] * 2
        args += [rope[0], rope[1]]
    out_specs = [pl.BlockSpec((None, tm, sp.width), lambda b, i: (b, i, 0)) for sp in splits]
    out_shape = [jax.ShapeDtypeStruct((bsz, rows, sp.width), sp.dtype) for sp in splits]
    out_specs.append(pl.BlockSpec((None, v_width, tm), lambda b, i: (b, 0, i)))
    out_shape.append(jax.ShapeDtypeStruct((bsz, v_width, rows), BF16))
    return pl.pallas_call(
        functools.partial(_inproj_kernel, splits=tuple(splits), use_norm=use_norm, use_rope=use_rope),
        grid=(bsz, rows // tm),
        in_specs=in_specs,
        out_specs=out_specs,
        out_shape=out_shape,
        compiler_params=_params("parallel", "parallel"),
        name="inproj",
    )(*args)


def _head_qts(qp):
    qt = qp.astype(F32).T
    zeros = jnp.zeros((HEAD_DIM, qp.shape[0]), BF16)
    return [jnp.concatenate([qt[:HEAD_DIM].astype(BF16), zeros], axis=0),
            jnp.concatenate([zeros, qt[HEAD_DIM:].astype(BF16)], axis=0)]


def _softmax_pv_step(state, s, vt):
    cm = jnp.max(s, axis=0, keepdims=True)
    if state is None:
        return cm, _dot(vt, jnp.exp2(s - cm).astype(BF16))
    m, o = state
    m_new = jnp.maximum(m, cm)
    return m_new, jnp.exp2(m - m_new) * o + _dot(vt, jnp.exp2(s - m_new).astype(BF16))


def _attend(groups, k_refs, vt_refs, finish):
    chunks = [(k_ref, vt_ref, slice(c * KEY_CHUNK, (c + 1) * KEY_CHUNK))
              for k_ref, vt_ref in zip(k_refs, vt_refs) for c in range(k_ref.shape[0] // KEY_CHUNK)]
    items = [(g, j) for g in range(len(groups)) for j in range(len(chunks))]

    def scores(item):
        g, j = item
        k_ref, _, ks = chunks[j]
        kc = k_ref[ks, :]
        return [_dot(kc, qt) for qt, _ in groups[g]]

    ones = jnp.ones((BF16_SUBLANES, KEY_CHUNK), BF16)
    pending = [scores(item) for item in items[:QK_LOOKAHEAD]]
    for t, (g, j) in enumerate(items):
        s_cur = pending.pop(0)
        if t + QK_LOOKAHEAD < len(items):
            pending.append(scores(items[t + QK_LOOKAHEAD]))
        if j == 0:
            state = [None] * len(groups[g])
        _, vt_ref, ks = chunks[j]
        state = [_softmax_pv_step(st, s, jnp.concatenate([vt_ref[v_rows, ks], ones], axis=0))
                 for st, s, (_, v_rows) in zip(state, s_cur, groups[g])]
        if j == len(chunks) - 1:
            finish(g, [(o[:-BF16_SUBLANES], o[-BF16_SUBLANES:-BF16_SUBLANES + 1]) for _, o in state])


def _attn_gqa_kernel(q_ref, *refs, n_parts):
    k_refs = refs[:n_parts]
    vt_refs = refs[n_parts:2 * n_parts]
    o_ref = refs[2 * n_parts]
    groups, places = [], []
    for r in range(q_ref.shape[0] // Q_TILE):
        for pr in range(q_ref.shape[1] // LANES):
            place = (slice(r * Q_TILE, (r + 1) * Q_TILE), slice(pr * LANES, (pr + 1) * LANES))
            groups.append([(qt, slice(half * HEAD_DIM, (half + 1) * HEAD_DIM))
                           for half, qt in enumerate(_head_qts(q_ref[place]))])
            places.append(place)

    def finish(g, pair):
        o_pair = jnp.concatenate([o / l for o, l in pair], axis=0)
        o_ref[places[g]] = o_pair.T.astype(o_ref.dtype)

    _attend(groups, k_refs, vt_refs, finish)


def _attn_gqa(q, ks, vts, *, tq):
    bsz, rows, _ = q.shape
    n_parts = len(ks)
    qw = A_WIDTH // 2
    k_specs = [pl.BlockSpec((None, k.shape[1], LANES), lambda b, p, i: (b, 0, p)) for k in ks]
    vt_specs = [pl.BlockSpec((None, LANES, vt.shape[2]), lambda b, p, i: (b, p, 0)) for vt in vts]
    return pl.pallas_call(
        functools.partial(_attn_gqa_kernel, n_parts=n_parts),
        grid=(bsz, 2, rows // tq),
        in_specs=[pl.BlockSpec((None, tq, qw), lambda b, p, i: (b, i, p))] + k_specs + vt_specs,
        out_specs=pl.BlockSpec((None, tq, qw), lambda b, p, i: (b, i, p)),
        out_shape=jax.ShapeDtypeStruct((bsz, rows, A_WIDTH), BF16),
        compiler_params=_params("parallel", "parallel", "parallel"),
        name="attn_gqa",
    )(q, *ks, *vts)


def _attn_diff_kernel(lam_ref, gain_ref, q_ref, *refs, n_parts, lam_init):
    k_refs = refs[:n_parts]
    vt_refs = refs[n_parts:2 * n_parts]
    o_ref = refs[2 * n_parts]
    lv = lam_ref[...]
    lam = (jnp.exp(jnp.sum(lv[0:1] * lv[1:2], axis=1, keepdims=True))
           - jnp.exp(jnp.sum(lv[2:3] * lv[3:4], axis=1, keepdims=True)) + lam_init)
    rows = slice(0, C_V_DIM)
    groups = []
    for r in range(q_ref.shape[0] // Q_TILE):
        groups.append([(qt, rows) for qt in _head_qts(q_ref[r * Q_TILE:(r + 1) * Q_TILE, :])])

    def finish(g, comps):
        (o0, l0), (o1, l1) = comps
        o = o0 * (1.0 / l0) - o1 * (lam / l1)
        ms = jnp.mean(o * o, axis=0, keepdims=True)
        o = (o * lax.rsqrt(ms + EPS)).T
        o_ref[g * Q_TILE:(g + 1) * Q_TILE, :] = (o * gain_ref[...] * (1.0 - lam_init)).astype(o_ref.dtype)

    _attend(groups, k_refs, vt_refs, finish)


def _attn_diff(lam_vecs, gain, q, ks, vts, *, tq, lam_init):
    bsz, rows, _ = q.shape
    n_parts = len(ks)
    heads = C_WIDTH // C_V_DIM
    k_specs = [pl.BlockSpec((None, k.shape[1], LANES), lambda b, h, i: (b, 0, h)) for k in ks]
    vt_specs = [pl.BlockSpec((None, C_V_DIM, vt.shape[2]), lambda b, h, i: (b, h, 0)) for vt in vts]
    return pl.pallas_call(
        functools.partial(_attn_diff_kernel, n_parts=n_parts, lam_init=lam_init),
        grid=(bsz, heads, rows // tq),
        in_specs=[pl.BlockSpec(lam_vecs.shape, lambda b, h, i: (0, 0)),
                  pl.BlockSpec(gain.shape, lambda b, h, i: (0, 0)),
                  pl.BlockSpec((None, tq, LANES), lambda b, h, i: (b, i, h))] + k_specs + vt_specs,
        out_specs=pl.BlockSpec((None, tq, LANES), lambda b, h, i: (b, i, h)),
        out_shape=jax.ShapeDtypeStruct((bsz, rows, C_WIDTH), BF16),
        compiler_params=_params("parallel", "parallel", "parallel"),
        name="attn_diff",
    )(lam_vecs, gain, q, *ks, *vts)


def _pool_kernel(u_ref, w_ref, ps_ref, o_ref, sh_ref, *, chunk):
    t_len = u_ref.shape[0]
    padded = t_len + 2 * POOL_PAD
    zeros = jnp.zeros((POOL_PAD, B_WIDTH), F32)
    sh_ref[0, 0:POOL_PAD, :] = zeros
    sh_ref[0, POOL_PAD + t_len:padded, :] = zeros
    sh_ref[0, POOL_PAD:POOL_PAD + t_len, :] = u_ref[...]
    shifted_len = padded - SUBLANES
    for r in range(1, SUBLANES):
        for c0 in range(0, shifted_len, chunk):
            n = min(chunk, shifted_len - c0)
            sh_ref[r, c0:c0 + n, :] = sh_ref[0, c0 + r:c0 + r + n, :]
    grp = lax.broadcasted_iota(jnp.int32, (chunk, B_WIDTH), 1) // B_GROUP_DIM
    for c in range(t_len // chunk):
        r0 = c * chunk

        def win(j):
            off = POOL_PAD + j
            lo = r0 + off // SUBLANES * SUBLANES
            return sh_ref[off % SUBLANES, lo:lo + chunk, :]

        tok = win(0)
        sums = []
        acc = None
        for w in POOL_WINDOWS:
            lo, hi = -(w // 2), w - 1 - w // 2
            if acc is None:
                acc = functools.reduce(jnp.add, [win(j) for j in range(lo, hi + 1)])
            else:
                plo, phi = -(prev_w // 2), prev_w - 1 - prev_w // 2
                acc = functools.reduce(jnp.add, [acc] + [win(j) for j in range(lo, plo)]
                                       + [win(j) for j in range(phi + 1, hi + 1)])
            prev_w = w
            sums.append(acc)
        t = lax.broadcasted_iota(jnp.int32, (chunk, B_WIDTH), 0) + r0
        half = jnp.where(grp == 0, 1, jnp.where(grp == 1, 2, jnp.where(grp == 2, 4, 8)))
        tail = half - 1
        cnt = (jnp.minimum(t + tail + 1, t_len) - jnp.maximum(t - half, 0)).astype(F32)
        s = jnp.where(grp == 0, sums[0], jnp.where(grp == 1, sums[1], jnp.where(grp == 2, sums[2], sums[3])))
        pooled = s / cnt - tok
        mixed = _dot(pooled.astype(BF16), w_ref[...]) * ps_ref[...]
        o_ref[r0:r0 + chunk, :] = mixed.astype(o_ref.dtype)


def _pool(u, w_bd, pool_scale):
    bsz, t_len, _ = u.shape
    return pl.pallas_call(
        functools.partial(_pool_kernel, chunk=min(t_len, 256)),
        grid=(bsz,),
        in_specs=[pl.BlockSpec((None, t_len, B_WIDTH), lambda b: (b, 0, 0)),
                  pl.BlockSpec(w_bd.shape, lambda b: (0, 0)),
                  pl.BlockSpec(pool_scale.shape, lambda b: (0, 0))],
        out_specs=pl.BlockSpec((None, t_len, B_WIDTH), lambda b: (b, 0, 0)),
        out_shape=jax.ShapeDtypeStruct((bsz, t_len, B_WIDTH), BF16),
        scratch_shapes=[pltpu.VMEM((SUBLANES, t_len + 2 * POOL_PAD, B_WIDTH), F32)],
        compiler_params=_params("parallel"),
        name="pool_mixer",
    )(u, w_bd, pool_scale)


def _conv_kernel(u_ref, w_ref, b_ref, g_ref, beta_ref, o_ref, sh_ref, *, chunk):
    t_len = u_ref.shape[0]
    padded = t_len + 2 * CONV_PAD
    zeros = jnp.zeros((CONV_PAD, D_CH), F32)
    sh_ref[0, 0:CONV_PAD, :] = zeros
    sh_ref[0, CONV_PAD + t_len:padded, :] = zeros
    for c in range(t_len // chunk):
        rs = slice(c * chunk, (c + 1) * chunk)
        sh_ref[0, CONV_PAD + c * chunk:CONV_PAD + (c + 1) * chunk, :] = (
            u_ref[rs, 0:D_CH] * jax.nn.sigmoid(u_ref[rs, D_CH:2 * D_CH]))
    shifted_len = padded - SUBLANES
    for r in range(1, SUBLANES):
        for c0 in range(0, shifted_len, chunk):
            n = min(chunk, shifted_len - c0)
            sh_ref[r, c0:c0 + n, :] = sh_ref[0, c0 + r:c0 + r + n, :]
    first = CONV_PAD - CONV_WIDTH // 2
    for c in range(t_len // chunk):
        acc = None
        for j in range(CONV_WIDTH):
            off = first + j
            lo = c * chunk + off // SUBLANES * SUBLANES
            term = sh_ref[off % SUBLANES, lo:lo + chunk, :] * w_ref[j:j + 1, :]
            acc = term if acc is None else acc + term
        z = _layer_norm(acc + b_ref[...], g_ref[...], beta_ref[...])
        o_ref[c * chunk:(c + 1) * chunk, :] = jax.nn.silu(z).astype(o_ref.dtype)


def _conformer_conv(u, conv_w, conv_b, ln_g, ln_b):
    bsz, t_len, _ = u.shape
    vec = pl.BlockSpec((1, D_CH), lambda b: (0, 0))
    return pl.pallas_call(
        functools.partial(_conv_kernel, chunk=128),
        grid=(bsz,),
        in_specs=[pl.BlockSpec((None, t_len, 2 * D_CH), lambda b: (b, 0, 0)),
                  pl.BlockSpec(conv_w.shape, lambda b: (0, 0)), vec, vec, vec],
        out_specs=pl.BlockSpec((None, t_len, D_CH), lambda b: (b, 0, 0)),
        out_shape=jax.ShapeDtypeStruct((bsz, t_len, D_CH), BF16),
        scratch_shapes=[pltpu.VMEM((SUBLANES, t_len + 2 * CONV_PAD, D_CH), F32)],
        compiler_params=_params("parallel"),
        name="conformer_conv",
    )(u, conv_w, conv_b, ln_g, ln_b)


def _outproj_kernel(a_ref, m_ref, w_ref, x_ref, gate_ref, g_ref, b_ref, o_ref):
    wa = a_ref.shape[1]
    sub = min(ROW_TILE, x_ref.shape[0])
    n_sub = x_ref.shape[0] // sub

    def project(r):
        rs = slice(r * sub, (r + 1) * sub)
        return _dot(a_ref[rs, :], w_ref[0:wa, :]) + _dot(m_ref[rs, :], w_ref[wa:, :])

    y_next = project(0)
    for r in range(n_sub):
        y = y_next
        if r + 1 < n_sub:
            y_next = project(r + 1)
        rs = slice(r * sub, (r + 1) * sub)
        z = ALPHA * x_ref[rs, :] + gate_ref[...] * y
        o_ref[rs, :] = _layer_norm(z, g_ref[...], b_ref[...])


def _outproj(attn, mix, w_out, x, mod, ln_g, ln_b, *, tm, per_batch, ctx_row):
    bsz, rows, _ = x.shape
    vec = pl.BlockSpec((1, D_MODEL), lambda b, i: (0, 0))
    return pl.pallas_call(
        _outproj_kernel,
        grid=(bsz, rows // tm),
        in_specs=[pl.BlockSpec((None, tm, attn.shape[2]), lambda b, i: (b, i, 0)),
                  pl.BlockSpec((None, tm, mix.shape[2]), lambda b, i: (b, i, 0)),
                  pl.BlockSpec(w_out.shape, lambda b, i: (0, 0)),
                  pl.BlockSpec((None, tm, D_MODEL), lambda b, i: (b, i, 0)),
                  _mod_spec(2, per_batch, ctx_row), vec, vec],
        out_specs=pl.BlockSpec((None, tm, D_MODEL), lambda b, i: (b, i, 0)),
        out_shape=jax.ShapeDtypeStruct(x.shape, F32),
        compiler_params=_params("parallel", "parallel"),
        name="outproj_ln",
    )(attn, mix, w_out, x, mod, ln_g, ln_b)


def _ffn_kernel(x_ref, sc_ref, sh_ref, gate_ref, wi_ref, wo_ref, g_ref, b_ref, o_ref, acc_ref):
    h = (x_ref[...] * (1.0 + sc_ref[...]) + sh_ref[...]).astype(BF16)
    n_chunks = FFN_HIDDEN // FFN_CHUNK

    def up(c):
        lo = c * FFN_CHUNK
        return (_dot(h, wi_ref[:, lo:lo + FFN_CHUNK]),
                _dot(h, wi_ref[:, FFN_HIDDEN + lo:FFN_HIDDEN + lo + FFN_CHUNK]))

    pending = [up(c) for c in range(FFN_LOOKAHEAD)]
    for c in range(n_chunks):
        a, g = pending.pop(0)
        if c + FFN_LOOKAHEAD < n_chunks:
            pending.append(up(c + FFN_LOOKAHEAD))
        y = _dot((jax.nn.silu(g) * a).astype(BF16), wo_ref[c * FFN_CHUNK:(c + 1) * FFN_CHUNK, :])
        if c == 0:
            acc_ref[...] = y
        else:
            acc_ref[...] += y
    z = ALPHA * x_ref[...] + gate_ref[...] * acc_ref[...]
    o_ref[...] = _layer_norm(z, g_ref[...], b_ref[...])


def _ffn(x, mod, w_in, w_out, layer, ln_g, ln_b, *, tm, per_batch, ctx_row):
    bsz, rows, _ = x.shape
    vec = pl.BlockSpec((1, D_MODEL), lambda b, i: (0, 0))
    resident = dict(pipeline_mode=pl.Buffered(1))
    return pl.pallas_call(
        _ffn_kernel,
        grid=(bsz, rows // tm),
        in_specs=[pl.BlockSpec((None, tm, D_MODEL), lambda b, i: (b, i, 0)),
                  _mod_spec(4, per_batch, ctx_row),
                  _mod_spec(3, per_batch, ctx_row),
                  _mod_spec(5, per_batch, ctx_row),
                  pl.BlockSpec((None,) + w_in.shape[1:], lambda b, i: (layer, 0, 0), **resident),
                  pl.BlockSpec((None,) + w_out.shape[1:], lambda b, i: (layer, 0, 0), **resident),
                  vec, vec],
        out_specs=pl.BlockSpec((None, tm, D_MODEL), lambda b, i: (b, i, 0)),
        out_shape=jax.ShapeDtypeStruct(x.shape, F32),
        scratch_shapes=[pltpu.VMEM((tm, D_MODEL), F32)],
        compiler_params=_params("parallel", "parallel"),
        name="ffn_ln",
    )(x, mod, mod, mod, w_in, w_out, ln_g, ln_b)


def _rope_tables(n_tokens):
    rows = n_tokens // GRID_W
    row = jnp.repeat(jnp.arange(rows, dtype=F32), GRID_W)
    col = jnp.tile(jnp.arange(GRID_W, dtype=F32), rows)
    axis_dim = HEAD_DIM // 2
    freqs = ROPE_THETA ** (-jnp.arange(0, axis_dim, 2, dtype=F32) / axis_dim)
    ang_r, ang_c = row[:, None] * freqs, col[:, None] * freqs
    cos = jnp.concatenate([jnp.cos(ang_r)] * 2 + [jnp.cos(ang_c)] * 2, axis=-1)
    sin = jnp.concatenate([-jnp.sin(ang_r), jnp.sin(ang_r), -jnp.sin(ang_c), jnp.sin(ang_c)], axis=-1)
    return jnp.tile(cos, (1, 2)), jnp.tile(sin, (1, 2))


def _pair_heads_cols(w):
    return w.reshape(w.shape[0], 2, 2, 3, HEAD_DIM).transpose(0, 1, 3, 2, 4).reshape(w.shape[0], A_WIDTH)


def _pair_heads_rows(w):
    return w.reshape(2, 2, 3, HEAD_DIM, w.shape[1]).transpose(0, 2, 1, 3, 4).reshape(A_WIDTH, w.shape[1])


def _block_diag(w_pool):
    out = jnp.zeros((B_WIDTH, B_WIDTH), w_pool.dtype)
    for g in range(B_GROUPS):
        sl = slice(g * B_GROUP_DIM, (g + 1) * B_GROUP_DIM)
        out = out.at[sl, sl].set(w_pool[g])
    return out


def kernel(x, c, ctx, c_ctx, ab_w_in, ab_q_gain, ab_k_gain, ab_w_pool, ab_pool_scale, ab_w_out,
           cd_w_in, cd_lambda_q1, cd_lambda_k1, cd_lambda_q2, cd_lambda_k2, cd_subln_gain,
           cd_conv_w, cd_conv_b, cd_conv_ln_g, cd_conv_ln_b, cd_w_out,
           ada_w, ada_b, ln1_g, ln1_b, ln2_g, ln2_b, ffn_w_in, ffn_w_out):
    bsz, t_len, _ = x.shape
    n_ctx = ctx.shape[1]
    ctx_row = bsz
    mod_rows = -(-(bsz + 1) // 8) * 8
    cc = jnp.zeros((mod_rows, D_MODEL), F32).at[:bsz].set(c).at[bsz].set(c_ctx)
    mods = _modulation(cc, ada_w, ada_b)
    rope = _rope_tables(t_len)
    row = lambda v: v.reshape(1, -1)
    head_of = jnp.arange(MXU_DIM) // HEAD_DIM
    gmat = jnp.where(head_of[:, None] == head_of[None, :], 1.0 / HEAD_DIM, 0.0).astype(BF16)
    w_ffn_in, w_ffn_out = ffn_w_in.astype(BF16), ffn_w_out.astype(BF16)

    l, i = 0, 0
    v0 = A_WIDTH + A_KV_WIDTH
    w_raw, wo_raw = ab_w_in[i].astype(BF16), ab_w_out[i].astype(BF16)
    w_in = jnp.concatenate([_pair_heads_cols(w_raw[:, :A_WIDTH]), w_raw[:, A_WIDTH:]], axis=1)
    w_vt = w_raw[:, v0:v0 + A_KV_WIDTH].T
    w_out = jnp.concatenate([_pair_heads_rows(wo_raw[:A_WIDTH]), wo_raw[A_WIDTH:]], axis=0)
    gain = jnp.concatenate([jnp.tile(ab_q_gain[i], A_WIDTH // HEAD_DIM),
                            jnp.tile(ab_k_gain[i], A_KV_WIDTH // HEAD_DIM)]).reshape(1, -1)
    splits = [Split(0, A_WIDTH, BF16, Q_SCALE, True, True),
              Split(A_WIDTH, A_KV_WIDTH, BF16, 1.0, True, True),
              Split(v0 + A_KV_WIDTH, B_WIDTH, F32, 1.0, False, False)]
    w_bd = _block_diag(ab_w_pool[i]).astype(BF16)
    pool_scale = row(ab_pool_scale[i])
    g1, b1, g2, b2 = row(ln1_g[l]), row(ln1_b[l]), row(ln2_g[l]), row(ln2_b[l])

    q, k, u, vt = _inproj(x, mods[l], w_in, w_vt, splits, tm=PROJ_TM, per_batch=True, ctx_row=ctx_row,
                          gain=gain, gmat=gmat, rope=rope)
    qc, kc, uc, vtc = _inproj(ctx, mods[l], w_in, w_vt, splits, tm=n_ctx, per_batch=False, ctx_row=ctx_row,
                              gain=gain, gmat=gmat)
    o = _attn_gqa(q, [kc, k], [vtc, vt], tq=GQA_TQ)
    oc = _attn_gqa(qc, [kc], [vtc], tq=n_ctx)
    mix = _pool(u, w_bd, pool_scale)
    mixc = _pool(uc, w_bd, pool_scale)
    x = _outproj(o, mix, w_out, x, mods[l], g1, b1, tm=PROJ_TM, per_batch=True, ctx_row=ctx_row)
    xc = _outproj(oc, mixc, w_out, ctx, mods[l], g1, b1, tm=n_ctx, per_batch=False, ctx_row=ctx_row)
    x = _ffn(x, mods[l], w_ffn_in, w_ffn_out, l, g2, b2, tm=FFN_TM, per_batch=True, ctx_row=ctx_row)
    xc = _ffn(xc.reshape(1, bsz * n_ctx, D_MODEL), mods[l], w_ffn_in, w_ffn_out, l, g2, b2,
              tm=FFN_TM, per_batch=False, ctx_row=ctx_row).reshape(bsz, n_ctx, D_MODEL)

    l, i = 1, 0
    lam_init = 0.8 - 0.6 * math.exp(-0.3 * l)
    w_in = cd_w_in[i].astype(BF16)
    w_vt = w_in[:, 2 * C_WIDTH:3 * C_WIDTH].T
    w_out = cd_w_out[i].astype(BF16)
    splits = [Split(0, C_WIDTH, BF16, Q_SCALE, False, True),
              Split(C_WIDTH, C_WIDTH, BF16, 1.0, False, True),
              Split(3 * C_WIDTH, 2 * D_CH, F32, 1.0, False, False)]
    ctx_splits = [Split(C_WIDTH, C_WIDTH, BF16, 1.0, False, False)]
    lam_vecs = jnp.stack([cd_lambda_q1[i], cd_lambda_k1[i], cd_lambda_q2[i], cd_lambda_k2[i]])
    sub_gain = row(cd_subln_gain[i])
    g1, b1, g2, b2 = row(ln1_g[l]), row(ln1_b[l]), row(ln2_g[l]), row(ln2_b[l])

    q, k, u, vt = _inproj(x, mods[l], w_in, w_vt, splits, tm=PROJ_TM, per_batch=True, ctx_row=ctx_row, rope=rope)
    kc, vtc = _inproj(xc, mods[l], w_in, w_vt, ctx_splits, tm=n_ctx, per_batch=False, ctx_row=ctx_row)
    o = _attn_diff(lam_vecs, sub_gain, q, [kc, k], [vtc, vt], tq=DIFF_TQ, lam_init=lam_init)
    conv = _conformer_conv(u, cd_conv_w[i], row(cd_conv_b[i]), row(cd_conv_ln_g[i]), row(cd_conv_ln_b[i]))
    x = _outproj(o, conv, w_out, x, mods[l], g1, b1, tm=PROJ_TM, per_batch=True, ctx_row=ctx_row)
    x = _ffn(x, mods[l], w_ffn_in, w_ffn_out, l, g2, b2, tm=FFN_TM, per_batch=True, ctx_row=ctx_row)
    return x
```

```python
import functools
import math
from typing import NamedTuple

import jax
import jax.numpy as jnp
from jax import lax
from jax.experimental import pallas as pl
from jax.experimental.pallas import tpu as pltpu

F32 = jnp.float32
BF16 = jnp.bfloat16

D_MODEL = 1024
DEPTH = 2
GRID_W = 64
HEAD_DIM = 64
ROPE_THETA = 10000.0
EPS = 1e-6

B_WIDTH = D_MODEL // 4
B_GROUPS = 4
B_GROUP_DIM = B_WIDTH // B_GROUPS
POOL_WINDOWS = (2, 4, 8, 16)
A_WIDTH = D_MODEL - B_WIDTH
A_KV_WIDTH = 4 * HEAD_DIM
D_CH = D_MODEL // 4
C_WIDTH = D_MODEL - D_CH
C_V_DIM = 2 * HEAD_DIM
CONV_WIDTH = 31
FFN_HIDDEN = -(-8 * D_MODEL // (3 * 256)) * 256
ALPHA = (2.0 * DEPTH) ** 0.25

LANES = 128
SUBLANES = 8
BF16_SUBLANES = 16
Q_SCALE = HEAD_DIM ** -0.5 * math.log2(math.e)
MXU_DIM = 256
VMEM_LIMIT = 48 * 1024 * 1024

POOL_PAD = 8
CONV_PAD = 16
FFN_CHUNK = 256
FFN_LOOKAHEAD = 1
KEY_CHUNK = MXU_DIM
QK_LOOKAHEAD = 2
ROW_TILE = MXU_DIM
PROJ_TM = 1024
FFN_TM = 512
Q_TILE = MXU_DIM
GQA_TQ = 1024
DIFF_TQ = 2048


def _params(*semantics):
    return pltpu.CompilerParams(dimension_semantics=semantics, vmem_limit_bytes=VMEM_LIMIT)


def _layer_norm(z, g, b):
    mu = jnp.mean(z, axis=-1, keepdims=True)
    zc = z - mu
    var = jnp.mean(zc * zc, axis=-1, keepdims=True)
    return zc * lax.rsqrt(var + EPS) * g + b


def _dot(a, b):
    return jnp.dot(a, b, preferred_element_type=F32)


def _dot_nt(a, b):
    return lax.dot_general(a, b, (((1,), (1,)), ((), ())), preferred_element_type=F32)


def _mod_kernel(c_ref, w_ref, b_ref, o_ref):
    s = jax.nn.silu(c_ref[...])
    o_ref[...] = jnp.dot(s, w_ref[...], preferred_element_type=F32,
                         precision=lax.Precision.HIGHEST) + b_ref[...]


def _modulation(cc, ada_w, ada_b):
    rows = cc.shape[0]
    tn = 1536
    out = pl.pallas_call(
        _mod_kernel,
        grid=(DEPTH, 6 * D_MODEL // tn),
        in_specs=[
            pl.BlockSpec((rows, D_MODEL), lambda l, j: (0, 0)),
            pl.BlockSpec((None, D_MODEL, tn), lambda l, j: (l, 0, j)),
            pl.BlockSpec((None, 1, tn), lambda l, j: (l, 0, j)),
        ],
        out_specs=pl.BlockSpec((None, rows, tn), lambda l, j: (l, 0, j)),
        out_shape=jax.ShapeDtypeStruct((DEPTH, rows, 6 * D_MODEL), F32),
        compiler_params=_params("parallel", "parallel"),
        name="adaln_mod",
    )(cc, ada_w, ada_b.reshape(DEPTH, 1, 6 * D_MODEL))
    return out.reshape(DEPTH, rows, 1, 6 * D_MODEL)


def _mod_spec(chunk, per_batch, ctx_row):
    if per_batch:
        return pl.BlockSpec((None, 1, D_MODEL), lambda b, *_: (b, 0, chunk))
    return pl.BlockSpec((None, 1, D_MODEL), lambda b, *_: (ctx_row, 0, chunk))


class Split(NamedTuple):
    start: int
    width: int
    dtype: object
    scale: float
    norm: bool
    rope: bool
    transposed: bool = False


def _group_rms_norm(p, gain, gmat):
    ss = p * p
    hi = ss.astype(BF16)
    lo = (ss - hi.astype(F32)).astype(BF16)
    blocks = []
    for j in range(p.shape[1] // MXU_DIM):
        sl = slice(j * MXU_DIM, (j + 1) * MXU_DIM)
        blocks.append(_dot(hi[:, sl], gmat) + _dot(lo[:, sl], gmat))
    ms = blocks[0] if len(blocks) == 1 else jnp.concatenate(blocks, axis=1)
    return p * lax.rsqrt(ms + EPS) * gain


def _rope(p, cos, sin):
    tm = p.shape[0]
    lane = lax.broadcasted_iota(jnp.int32, (tm, LANES), 1)
    first = (lane & 16) == 0
    outs = []
    for j in range(p.shape[1] // LANES):
        xb = p[:, j * LANES:(j + 1) * LANES]
        partner = jnp.where(first, pltpu.roll(xb, LANES - 16, 1), pltpu.roll(xb, 16, 1))
        outs.append(xb * cos + partner * sin)
    return jnp.concatenate(outs, axis=1)


def _inproj_kernel(x_ref, sc_ref, sh_ref, w_ref, *rest, splits, use_norm, use_rope):
    idx = 0
    if use_norm:
        gain_ref, gmat_ref = rest[0], rest[1]
        idx = 2
    if use_rope:
        cos_ref, sin_ref = rest[idx], rest[idx + 1]
        idx += 2
    out_refs = rest[idx:]
    sub = min(ROW_TILE, x_ref.shape[0])
    n_sub = x_ref.shape[0] // sub

    def project(r):
        rs = slice(r * sub, (r + 1) * sub)
        h = (x_ref[rs, :] * (1.0 + sc_ref[...]) + sh_ref[...]).astype(BF16)
        return [_dot(h, w_ref[:, sp.start:sp.start + sp.width]) for sp in splits]

    def finish(r, ps):
        rs = slice(r * sub, (r + 1) * sub)
        gain_off = 0
        for sp, p, o_ref in zip(splits, ps, out_refs):
            if sp.norm:
                p = _group_rms_norm(p, gain_ref[:, gain_off:gain_off + sp.width], gmat_ref[...])
                gain_off += sp.width
            if sp.rope and use_rope:
                p = _rope(p, cos_ref[rs, :], sin_ref[rs, :])
            if sp.scale != 1.0:
                p = p * sp.scale
            if sp.transposed:
                o_ref[:, rs] = p.T.astype(o_ref.dtype)
            else:
                o_ref[rs, :] = p.astype(o_ref.dtype)

    ps_next = project(0)
    for r in range(n_sub):
        ps = ps_next
        if r + 1 < n_sub:
            ps_next = project(r + 1)
        finish(r, ps)


def _inproj(x, mod, w, splits, *, tm, per_batch, ctx_row, gain=None, gmat=None, rope=None):
    bsz, rows, _ = x.shape
    n_cols = w.shape[1]
    use_norm = gain is not None
    use_rope = rope is not None
    in_specs = [
        pl.BlockSpec((None, tm, D_MODEL), lambda b, i: (b, i, 0)),
        _mod_spec(1, per_batch, ctx_row),
        _mod_spec(0, per_batch, ctx_row),
        pl.BlockSpec((D_MODEL, n_cols), lambda b, i: (0, 0)),
    ]
    args = [x, mod, mod, w]
    if use_norm:
        in_specs += [pl.BlockSpec(gain.shape, lambda b, i: (0, 0)),
                     pl.BlockSpec(gmat.shape, lambda b, i: (0, 0))]
        args += [gain, gmat]
    if use_rope:
        in_specs += [pl.BlockSpec((tm, LANES), lambda b, i: (i, 0))] * 2
        args += [rope[0], rope[1]]
    out_specs = [pl.BlockSpec((None, sp.width, tm), lambda b, i: (b, 0, i)) if sp.transposed
                 else pl.BlockSpec((None, tm, sp.width), lambda b, i: (b, i, 0)) for sp in splits]
    out_shape = [jax.ShapeDtypeStruct((bsz, sp.width, rows) if sp.transposed else (bsz, rows, sp.width), sp.dtype)
                 for sp in splits]
    return pl.pallas_call(
        functools.partial(_inproj_kernel, splits=tuple(splits), use_norm=use_norm, use_rope=use_rope),
        grid=(bsz, rows // tm),
        in_specs=in_specs,
        out_specs=out_specs,
        out_shape=out_shape,
        compiler_params=_params("parallel", "parallel"),
        name="inproj",
    )(*args)


def _head_qts(qp):
    qt = qp.astype(F32).T
    zeros = jnp.zeros((HEAD_DIM, qp.shape[0]), BF16)
    return [jnp.concatenate([qt[:HEAD_DIM].astype(BF16), zeros], axis=0),
            jnp.concatenate([zeros, qt[HEAD_DIM:].astype(BF16)], axis=0)]


def _softmax_pv_step(state, s, vt):
    cm = jnp.max(s, axis=0, keepdims=True)
    if state is None:
        return cm, _dot(vt, jnp.exp2(s - cm).astype(BF16))
    m, o = state
    m_new = jnp.maximum(m, cm)
    return m_new, jnp.exp2(m - m_new) * o + _dot(vt, jnp.exp2(s - m_new).astype(BF16))


def _attend(groups, k_refs, vt_refs, finish):
    chunks = [(k_ref, vt_ref, slice(c * KEY_CHUNK, (c + 1) * KEY_CHUNK))
              for k_ref, vt_ref in zip(k_refs, vt_refs) for c in range(k_ref.shape[0] // KEY_CHUNK)]
    items = [(g, j) for g in range(len(groups)) for j in range(len(chunks))]

    def scores(item):
        g, j = item
        k_ref, _, ks = chunks[j]
        kc = k_ref[ks, :]
        return [_dot(kc, qt) for qt, _ in groups[g]]

    ones = jnp.ones((BF16_SUBLANES, KEY_CHUNK), BF16)
    pending = [scores(item) for item in items[:QK_LOOKAHEAD]]
    for t, (g, j) in enumerate(items):
        s_cur = pending.pop(0)
        if t + QK_LOOKAHEAD < len(items):
            pending.append(scores(items[t + QK_LOOKAHEAD]))
        if j == 0:
            state = [None] * len(groups[g])
        _, vt_ref, ks = chunks[j]
        state = [_softmax_pv_step(st, s, jnp.concatenate([vt_ref[v_rows, ks], ones], axis=0))
                 for st, s, (_, v_rows) in zip(state, s_cur, groups[g])]
        if j == len(chunks) - 1:
            finish(g, [(o[:-BF16_SUBLANES], o[-BF16_SUBLANES:-BF16_SUBLANES + 1]) for _, o in state])


def _attn_gqa_kernel(q_ref, *refs, n_parts):
    k_refs = refs[:n_parts]
    vt_refs = refs[n_parts:2 * n_parts]
    o_ref = refs[2 * n_parts]
    groups, places = [], []
    for r in range(q_ref.shape[0] // Q_TILE):
        for pr in range(q_ref.shape[1] // LANES):
            place = (slice(r * Q_TILE, (r + 1) * Q_TILE), slice(pr * LANES, (pr + 1) * LANES))
            groups.append([(qt, slice(half * HEAD_DIM, (half + 1) * HEAD_DIM))
                           for half, qt in enumerate(_head_qts(q_ref[place]))])
            places.append(place)

    def finish(g, pair):
        o_pair = jnp.concatenate([o / l for o, l in pair], axis=0)
        o_ref[places[g]] = o_pair.T.astype(o_ref.dtype)

    _attend(groups, k_refs, vt_refs, finish)


def _attn_gqa(q, ks, vts, *, tq):
    bsz, rows, _ = q.shape
    n_parts = len(ks)
    qw = A_WIDTH // 2
    k_specs = [pl.BlockSpec((None, k.shape[1], LANES), lambda b, p, i: (b, 0, p)) for k in ks]
    vt_specs = [pl.BlockSpec((None, LANES, vt.shape[2]), lambda b, p, i: (b, p, 0)) for vt in vts]
    return pl.pallas_call(
        functools.partial(_attn_gqa_kernel, n_parts=n_parts),
        grid=(bsz, 2, rows // tq),
        in_specs=[pl.BlockSpec((None, tq, qw), lambda b, p, i: (b, i, p))] + k_specs + vt_specs,
        out_specs=pl.BlockSpec((None, tq, qw), lambda b, p, i: (b, i, p)),
        out_shape=jax.ShapeDtypeStruct((bsz, rows, A_WIDTH), BF16),
        compiler_params=_params("parallel", "parallel", "parallel"),
        name="attn_gqa",
    )(q, *ks, *vts)


def _attn_diff_kernel(lam_ref, gain_ref, q_ref, *refs, n_parts, lam_init):
    k_refs = refs[:n_parts]
    vt_refs = refs[n_parts:2 * n_parts]
    o_ref = refs[2 * n_parts]
    lv = lam_ref[...]
    lam = (jnp.exp(jnp.sum(lv[0:1] * lv[1:2], axis=1, keepdims=True))
           - jnp.exp(jnp.sum(lv[2:3] * lv[3:4], axis=1, keepdims=True)) + lam_init)
    rows = slice(0, C_V_DIM)
    groups = []
    for r in range(q_ref.shape[0] // Q_TILE):
        groups.append([(qt, rows) for qt in _head_qts(q_ref[r * Q_TILE:(r + 1) * Q_TILE, :])])

    def finish(g, comps):
        (o0, l0), (o1, l1) = comps
        o = o0 * (1.0 / l0) - o1 * (lam / l1)
        ms = jnp.mean(o * o, axis=0, keepdims=True)
        o = (o * lax.rsqrt(ms + EPS)).T
        o_ref[g * Q_TILE:(g + 1) * Q_TILE, :] = (o * gain_ref[...] * (1.0 - lam_init)).astype(o_ref.dtype)

    _attend(groups, k_refs, vt_refs, finish)


def _attn_diff(lam_vecs, gain, q, ks, vts, *, tq, lam_init):
    bsz, rows, _ = q.shape
    n_parts = len(ks)
    heads = C_WIDTH // C_V_DIM
    k_specs = [pl.BlockSpec((None, k.shape[1], LANES), lambda b, h, i: (b, 0, h)) for k in ks]
    vt_specs = [pl.BlockSpec((None, C_V_DIM, vt.shape[2]), lambda b, h, i: (b, h, 0)) for vt in vts]
    return pl.pallas_call(
        functools.partial(_attn_diff_kernel, n_parts=n_parts, lam_init=lam_init),
        grid=(bsz, heads, rows // tq),
        in_specs=[pl.BlockSpec(lam_vecs.shape, lambda b, h, i: (0, 0)),
                  pl.BlockSpec(gain.shape, lambda b, h, i: (0, 0)),
                  pl.BlockSpec((None, tq, LANES), lambda b, h, i: (b, i, h))] + k_specs + vt_specs,
        out_specs=pl.BlockSpec((None, tq, LANES), lambda b, h, i: (b, i, h)),
        out_shape=jax.ShapeDtypeStruct((bsz, rows, C_WIDTH), BF16),
        compiler_params=_params("parallel", "parallel", "parallel"),
        name="attn_diff",
    )(lam_vecs, gain, q, *ks, *vts)


def _pool_kernel(u_ref, w_ref, ps_ref, o_ref, sh_ref, *, chunk):
    t_len = u_ref.shape[0]
    padded = t_len + 2 * POOL_PAD
    zeros = jnp.zeros((POOL_PAD, B_WIDTH), F32)
    sh_ref[0, 0:POOL_PAD, :] = zeros
    sh_ref[0, POOL_PAD + t_len:padded, :] = zeros
    sh_ref[0, POOL_PAD:POOL_PAD + t_len, :] = u_ref[...]
    shifted_len = padded - SUBLANES
    for r in range(1, SUBLANES):
        for c0 in range(0, shifted_len, chunk):
            n = min(chunk, shifted_len - c0)
            sh_ref[r, c0:c0 + n, :] = sh_ref[0, c0 + r:c0 + r + n, :]
    grp = lax.broadcasted_iota(jnp.int32, (chunk, B_WIDTH), 1) // B_GROUP_DIM
    for c in range(t_len // chunk):
        r0 = c * chunk

        def win(j):
            off = POOL_PAD + j
            lo = r0 + off // SUBLANES * SUBLANES
            return sh_ref[off % SUBLANES, lo:lo + chunk, :]

        tok = win(0)
        sums = []
        acc = None
        for w in POOL_WINDOWS:
            lo, hi = -(w // 2), w - 1 - w // 2
            if acc is None:
                acc = functools.reduce(jnp.add, [win(j) for j in range(lo, hi + 1)])
            else:
                plo, phi = -(prev_w // 2), prev_w - 1 - prev_w // 2
                acc = functools.reduce(jnp.add, [acc] + [win(j) for j in range(lo, plo)]
                                       + [win(j) for j in range(phi + 1, hi + 1)])
            prev_w = w
            sums.append(acc)
        t = lax.broadcasted_iota(jnp.int32, (chunk, B_WIDTH), 0) + r0
        half = jnp.where(grp == 0, 1, jnp.where(grp == 1, 2, jnp.where(grp == 2, 4, 8)))
        tail = half - 1
        cnt = (jnp.minimum(t + tail + 1, t_len) - jnp.maximum(t - half, 0)).astype(F32)
        s = jnp.where(grp == 0, sums[0], jnp.where(grp == 1, sums[1], jnp.where(grp == 2, sums[2], sums[3])))
        pooled = s / cnt - tok
        mixed = _dot(pooled.astype(BF16), w_ref[...]) * ps_ref[...]
        o_ref[r0:r0 + chunk, :] = mixed.astype(o_ref.dtype)


def _pool(u, w_bd, pool_scale):
    bsz, t_len, _ = u.shape
    return pl.pallas_call(
        functools.partial(_pool_kernel, chunk=min(t_len, 256)),
        grid=(bsz,),
        in_specs=[pl.BlockSpec((None, t_len, B_WIDTH), lambda b: (b, 0, 0)),
                  pl.BlockSpec(w_bd.shape, lambda b: (0, 0)),
                  pl.BlockSpec(pool_scale.shape, lambda b: (0, 0))],
        out_specs=pl.BlockSpec((None, t_len, B_WIDTH), lambda b: (b, 0, 0)),
        out_shape=jax.ShapeDtypeStruct((bsz, t_len, B_WIDTH), BF16),
        scratch_shapes=[pltpu.VMEM((SUBLANES, t_len + 2 * POOL_PAD, B_WIDTH), F32)],
        compiler_params=_params("parallel"),
        name="pool_mixer",
    )(u, w_bd, pool_scale)


def _conv_kernel(u_ref, w_ref, b_ref, g_ref, beta_ref, o_ref, sh_ref, *, chunk):
    t_len = u_ref.shape[0]
    padded = t_len + 2 * CONV_PAD
    zeros = jnp.zeros((CONV_PAD, D_CH), F32)
    sh_ref[0, 0:CONV_PAD, :] = zeros
    sh_ref[0, CONV_PAD + t_len:padded, :] = zeros
    for c in range(t_len // chunk):
        rs = slice(c * chunk, (c + 1) * chunk)
        sh_ref[0, CONV_PAD + c * chunk:CONV_PAD + (c + 1) * chunk, :] = (
            u_ref[rs, 0:D_CH] * jax.nn.sigmoid(u_ref[rs, D_CH:2 * D_CH]))
    shifted_len = padded - SUBLANES
    for r in range(1, SUBLANES):
        for c0 in range(0, shifted_len, chunk):
            n = min(chunk, shifted_len - c0)
            sh_ref[r, c0:c0 + n, :] = sh_ref[0, c0 + r:c0 + r + n, :]
    first = CONV_PAD - CONV_WIDTH // 2
    for c in range(t_len // chunk):
        acc = None
        for j in range(CONV_WIDTH):
            off = first + j
            lo = c * chunk + off // SUBLANES * SUBLANES
            term = sh_ref[off % SUBLANES, lo:lo + chunk, :] * w_ref[j:j + 1, :]
            acc = term if acc is None else acc + term
        z = _layer_norm(acc + b_ref[...], g_ref[...], beta_ref[...])
        o_ref[c * chunk:(c + 1) * chunk, :] = jax.nn.silu(z).astype(o_ref.dtype)


def _conformer_conv(u, conv_w, conv_b, ln_g, ln_b):
    bsz, t_len, _ = u.shape
    vec = pl.BlockSpec((1, D_CH), lambda b: (0, 0))
    return pl.pallas_call(
        functools.partial(_conv_kernel, chunk=128),
        grid=(bsz,),
        in_specs=[pl.BlockSpec((None, t_len, 2 * D_CH), lambda b: (b, 0, 0)),
                  pl.BlockSpec(conv_w.shape, lambda b: (0, 0)), vec, vec, vec],
        out_specs=pl.BlockSpec((None, t_len, D_CH), lambda b: (b, 0, 0)),
        out_shape=jax.ShapeDtypeStruct((bsz, t_len, D_CH), BF16),
        scratch_shapes=[pltpu.VMEM((SUBLANES, t_len + 2 * CONV_PAD, D_CH), F32)],
        compiler_params=_params("parallel"),
        name="conformer_conv",
    )(u, conv_w, conv_b, ln_g, ln_b)


def _tail_kernel(a_ref, m_ref, wp_ref, x_ref, gate1_ref, g1_ref, b1_ref, sc_ref, sh_ref, gate_ref,
                 wi_ref, wo_ref, g_ref, b_ref, o_ref, x1_ref, acc_ref):
    wa = a_ref.shape[1]
    sub = min(ROW_TILE, x_ref.shape[0])
    n_sub = x_ref.shape[0] // sub

    def project(r):
        rs = slice(r * sub, (r + 1) * sub)
        return _dot(a_ref[rs, :], wp_ref[0:wa, :]) + _dot(m_ref[rs, :], wp_ref[wa:, :])

    y_next = project(0)
    for r in range(n_sub):
        y = y_next
        if r + 1 < n_sub:
            y_next = project(r + 1)
        rs = slice(r * sub, (r + 1) * sub)
        z = ALPHA * x_ref[rs, :] + gate1_ref[...] * y
        x1_ref[rs, :] = _layer_norm(z, g1_ref[...], b1_ref[...])

    h = (x1_ref[...] * (1.0 + sc_ref[...]) + sh_ref[...]).astype(BF16)
    n_chunks = FFN_HIDDEN // FFN_CHUNK

    def up(c):
        lo = c * FFN_CHUNK
        return (_dot(h, wi_ref[:, lo:lo + FFN_CHUNK]),
                _dot(h, wi_ref[:, FFN_HIDDEN + lo:FFN_HIDDEN + lo + FFN_CHUNK]))

    pending = [up(c) for c in range(FFN_LOOKAHEAD)]
    for c in range(n_chunks):
        a, g = pending.pop(0)
        if c + FFN_LOOKAHEAD < n_chunks:
            pending.append(up(c + FFN_LOOKAHEAD))
        y = _dot((jax.nn.silu(g) * a).astype(BF16), wo_ref[c * FFN_CHUNK:(c + 1) * FFN_CHUNK, :])
        if c == 0:
            acc_ref[...] = y
        else:
            acc_ref[...] += y
    z = ALPHA * x1_ref[...] + gate_ref[...] * acc_ref[...]
    o_ref[...] = _layer_norm(z, g_ref[...], b_ref[...])


def _tail(attn, mix, w_proj, x, mod, ln1_g, ln1_b, w_in, w_out, layer, ln2_g, ln2_b, *, tm, per_batch, ctx_row):
    bsz, rows, _ = x.shape
    vec = pl.BlockSpec((1, D_MODEL), lambda b, i: (0, 0))
    resident = dict(pipeline_mode=pl.Buffered(1))
    return pl.pallas_call(
        _tail_kernel,
        grid=(bsz, rows // tm),
        in_specs=[pl.BlockSpec((None, tm, attn.shape[2]), lambda b, i: (b, i, 0)),
                  pl.BlockSpec((None, tm, mix.shape[2]), lambda b, i: (b, i, 0)),
                  pl.BlockSpec(w_proj.shape, lambda b, i: (0, 0), **resident),
                  pl.BlockSpec((None, tm, D_MODEL), lambda b, i: (b, i, 0)),
                  _mod_spec(2, per_batch, ctx_row),
                  vec, vec,
                  _mod_spec(4, per_batch, ctx_row),
                  _mod_spec(3, per_batch, ctx_row),
                  _mod_spec(5, per_batch, ctx_row),
                  pl.BlockSpec((None,) + w_in.shape[1:], lambda b, i: (layer, 0, 0), **resident),
                  pl.BlockSpec((None,) + w_out.shape[1:], lambda b, i: (layer, 0, 0), **resident),
                  vec, vec],
        out_specs=pl.BlockSpec((None, tm, D_MODEL), lambda b, i: (b, i, 0)),
        out_shape=jax.ShapeDtypeStruct(x.shape, F32),
        scratch_shapes=[pltpu.VMEM((tm, D_MODEL), F32), pltpu.VMEM((tm, D_MODEL), F32)],
        compiler_params=_params("parallel", "parallel"),
        name="outproj_ffn",
    )(attn, mix, w_proj, x, mod, ln1_g, ln1_b, mod, mod, mod, w_in, w_out, ln2_g, ln2_b)


def _rope_tables(n_tokens):
    rows = n_tokens // GRID_W
    row = jnp.repeat(jnp.arange(rows, dtype=F32), GRID_W)
    col = jnp.tile(jnp.arange(GRID_W, dtype=F32), rows)
    axis_dim = HEAD_DIM // 2
    freqs = ROPE_THETA ** (-jnp.arange(0, axis_dim, 2, dtype=F32) / axis_dim)
    ang_r, ang_c = row[:, None] * freqs, col[:, None] * freqs
    cos = jnp.concatenate([jnp.cos(ang_r)] * 2 + [jnp.cos(ang_c)] * 2, axis=-1)
    sin = jnp.concatenate([-jnp.sin(ang_r), jnp.sin(ang_r), -jnp.sin(ang_c), jnp.sin(ang_c)], axis=-1)
    return jnp.tile(cos, (1, 2)), jnp.tile(sin, (1, 2))


def _pair_heads_cols(w):
    return w.reshape(w.shape[0], 2, 2, 3, HEAD_DIM).transpose(0, 1, 3, 2, 4).reshape(w.shape[0], A_WIDTH)


def _pair_heads_rows(w):
    return w.reshape(2, 2, 3, HEAD_DIM, w.shape[1]).transpose(0, 2, 1, 3, 4).reshape(A_WIDTH, w.shape[1])


def _block_diag(w_pool):
    out = jnp.zeros((B_WIDTH, B_WIDTH), w_pool.dtype)
    for g in range(B_GROUPS):
        sl = slice(g * B_GROUP_DIM, (g + 1) * B_GROUP_DIM)
        out = out.at[sl, sl].set(w_pool[g])
    return out


def kernel(x, c, ctx, c_ctx, ab_w_in, ab_q_gain, ab_k_gain, ab_w_pool, ab_pool_scale, ab_w_out,
           cd_w_in, cd_lambda_q1, cd_lambda_k1, cd_lambda_q2, cd_lambda_k2, cd_subln_gain,
           cd_conv_w, cd_conv_b, cd_conv_ln_g, cd_conv_ln_b, cd_w_out,
           ada_w, ada_b, ln1_g, ln1_b, ln2_g, ln2_b, ffn_w_in, ffn_w_out):
    bsz, t_len, _ = x.shape
    n_ctx = ctx.shape[1]
    ctx_row = bsz
    mod_rows = -(-(bsz + 1) // 8) * 8
    cc = jnp.zeros((mod_rows, D_MODEL), F32).at[:bsz].set(c).at[bsz].set(c_ctx)
    mods = _modulation(cc, ada_w, ada_b)
    rope = _rope_tables(t_len)
    row = lambda v: v.reshape(1, -1)
    head_of = jnp.arange(MXU_DIM) // HEAD_DIM
    gmat = jnp.where(head_of[:, None] == head_of[None, :], 1.0 / HEAD_DIM, 0.0).astype(BF16)
    w_ffn_in, w_ffn_out = ffn_w_in.astype(BF16), ffn_w_out.astype(BF16)

    l, i = 0, 0
    v0 = A_WIDTH + A_KV_WIDTH
    w_raw, wo_raw = ab_w_in[i].astype(BF16), ab_w_out[i].astype(BF16)
    w_in = jnp.concatenate([_pair_heads_cols(w_raw[:, :A_WIDTH]), w_raw[:, A_WIDTH:]], axis=1)
    w_out = jnp.concatenate([_pair_heads_rows(wo_raw[:A_WIDTH]), wo_raw[A_WIDTH:]], axis=0)
    gain = jnp.concatenate([jnp.tile(ab_q_gain[i], A_WIDTH // HEAD_DIM),
                            jnp.tile(ab_k_gain[i], A_KV_WIDTH // HEAD_DIM)]).reshape(1, -1)
    splits = [Split(0, A_WIDTH, BF16, Q_SCALE, True, True),
              Split(A_WIDTH, A_KV_WIDTH, BF16, 1.0, True, True),
              Split(v0 + A_KV_WIDTH, B_WIDTH, F32, 1.0, False, False),
              Split(v0, A_KV_WIDTH, BF16, 1.0, False, False, transposed=True)]
    w_bd = _block_diag(ab_w_pool[i]).astype(BF16)
    pool_scale = row(ab_pool_scale[i])
    g1, b1, g2, b2 = row(ln1_g[l]), row(ln1_b[l]), row(ln2_g[l]), row(ln2_b[l])

    q, k, u, vt = _inproj(x, mods[l], w_in, splits, tm=PROJ_TM, per_batch=True, ctx_row=ctx_row,
                          gain=gain, gmat=gmat, rope=rope)
    qc, kc, uc, vtc = _inproj(ctx, mods[l], w_in, splits, tm=n_ctx, per_batch=False, ctx_row=ctx_row,
                              gain=gain, gmat=gmat)
    o = _attn_gqa(q, [kc, k], [vtc, vt], tq=GQA_TQ)
    oc = _attn_gqa(qc, [kc], [vtc], tq=n_ctx)
    mix = _pool(u, w_bd, pool_scale)
    mixc = _pool(uc, w_bd, pool_scale)
    x = _tail(o, mix, w_out, x, mods[l], g1, b1, w_ffn_in, w_ffn_out, l, g2, b2,
              tm=FFN_TM, per_batch=True, ctx_row=ctx_row)
    flat = lambda t: t.reshape(1, bsz * n_ctx, t.shape[-1])
    xc = _tail(flat(oc), flat(mixc), w_out, flat(ctx), mods[l], g1, b1, w_ffn_in, w_ffn_out, l, g2, b2,
               tm=FFN_TM, per_batch=False, ctx_row=ctx_row).reshape(bsz, n_ctx, D_MODEL)

    l, i = 1, 0
    lam_init = 0.8 - 0.6 * math.exp(-0.3 * l)
    w_in = cd_w_in[i].astype(BF16)
    w_out = cd_w_out[i].astype(BF16)
    v_split = Split(2 * C_WIDTH, C_WIDTH, BF16, 1.0, False, False, transposed=True)
    splits = [Split(0, C_WIDTH, BF16, Q_SCALE, False, True),
              Split(C_WIDTH, C_WIDTH, BF16, 1.0, False, True),
              Split(3 * C_WIDTH, 2 * D_CH, F32, 1.0, False, False),
              v_split]
    ctx_splits = [Split(C_WIDTH, C_WIDTH, BF16, 1.0, False, False), v_split]
    lam_vecs = jnp.stack([cd_lambda_q1[i], cd_lambda_k1[i], cd_lambda_q2[i], cd_lambda_k2[i]])
    sub_gain = row(cd_subln_gain[i])
    g1, b1, g2, b2 = row(ln1_g[l]), row(ln1_b[l]), row(ln2_g[l]), row(ln2_b[l])

    q, k, u, vt = _inproj(x, mods[l], w_in, splits, tm=PROJ_TM, per_batch=True, ctx_row=ctx_row, rope=rope)
    kc, vtc = _inproj(xc, mods[l], w_in, ctx_splits, tm=n_ctx, per_batch=False, ctx_row=ctx_row)
    o = _attn_diff(lam_vecs, sub_gain, q, [kc, k], [vtc, vt], tq=DIFF_TQ, lam_init=lam_init)
    conv = _conformer_conv(u, cd_conv_w[i], row(cd_conv_b[i]), row(cd_conv_ln_g[i]), row(cd_conv_ln_b[i]))
    return _tail(o, conv, w_out, x, mods[l], g1, b1, w_ffn_in, w_ffn_out, l, g2, b2,
                 tm=FFN_TM, per_batch=True, ctx_row=ctx_row)
```

```python
import functools
import math
from typing import NamedTuple

import jax
import jax.numpy as jnp
from jax import lax
from jax.experimental import pallas as pl
from jax.experimental.pallas import tpu as pltpu

F32 = jnp.float32
BF16 = jnp.bfloat16

D_MODEL = 1024
DEPTH = 2
GRID_W = 64
HEAD_DIM = 64
ROPE_THETA = 10000.0
EPS = 1e-6

B_WIDTH = D_MODEL // 4
B_GROUPS = 4
B_GROUP_DIM = B_WIDTH // B_GROUPS
POOL_WINDOWS = (2, 4, 8, 16)
A_WIDTH = D_MODEL - B_WIDTH
A_KV_WIDTH = 4 * HEAD_DIM
D_CH = D_MODEL // 4
C_WIDTH = D_MODEL - D_CH
C_V_DIM = 2 * HEAD_DIM
CONV_WIDTH = 31
FFN_HIDDEN = -(-8 * D_MODEL // (3 * 256)) * 256
ALPHA = (2.0 * DEPTH) ** 0.25

LANES = 128
SUBLANES = 8
BF16_SUBLANES = 16
Q_SCALE = HEAD_DIM ** -0.5 * math.log2(math.e)
MXU_DIM = 256
VMEM_LIMIT = 48 * 1024 * 1024

POOL_PAD = 8
CONV_PAD = 16
FFN_CHUNK = 256
FFN_LOOKAHEAD = 1
KEY_CHUNK = MXU_DIM
QK_LOOKAHEAD = 2
ROW_TILE = MXU_DIM
PROJ_TM = 1024
FFN_TM = 512
Q_TILE = MXU_DIM
GQA_TQ = 1024
DIFF_TQ = 2048


def _params(*semantics):
    return pltpu.CompilerParams(dimension_semantics=semantics, vmem_limit_bytes=VMEM_LIMIT)


def _layer_norm(z, g, b):
    mu = jnp.mean(z, axis=-1, keepdims=True)
    zc = z - mu
    var = jnp.mean(zc * zc, axis=-1, keepdims=True)
    return zc * lax.rsqrt(var + EPS) * g + b


def _dot(a, b):
    return jnp.dot(a, b, preferred_element_type=F32)


def _dot_nt(a, b):
    return lax.dot_general(a, b, (((1,), (1,)), ((), ())), preferred_element_type=F32)


def _mod_kernel(c_ref, w_ref, b_ref, o_ref):
    s = jax.nn.silu(c_ref[...])
    w = w_ref[...]
    s_hi, w_hi = s.astype(BF16), w.astype(BF16)
    s_lo = (s - s_hi.astype(F32)).astype(BF16)
    w_lo = (w - w_hi.astype(F32)).astype(BF16)
    o_ref[...] = _dot(s_hi, w_hi) + _dot(s_lo, w_hi) + _dot(s_hi, w_lo) + b_ref[...]


def _modulation(cc, ada_w, ada_b):
    rows = cc.shape[0]
    tn = 1536
    out = pl.pallas_call(
        _mod_kernel,
        grid=(DEPTH, 6 * D_MODEL // tn),
        in_specs=[
            pl.BlockSpec((rows, D_MODEL), lambda l, j: (0, 0)),
            pl.BlockSpec((None, D_MODEL, tn), lambda l, j: (l, 0, j)),
            pl.BlockSpec((None, 1, tn), lambda l, j: (l, 0, j)),
        ],
        out_specs=pl.BlockSpec((None, rows, tn), lambda l, j: (l, 0, j)),
        out_shape=jax.ShapeDtypeStruct((DEPTH, rows, 6 * D_MODEL), F32),
        compiler_params=_params("parallel", "parallel"),
        name="adaln_mod",
    )(cc, ada_w, ada_b.reshape(DEPTH, 1, 6 * D_MODEL))
    return out.reshape(DEPTH, rows, 1, 6 * D_MODEL)


def _mod_spec(chunk, per_batch, ctx_row):
    if per_batch:
        return pl.BlockSpec((None, 1, D_MODEL), lambda b, *_: (b, 0, chunk))
    return pl.BlockSpec((None, 1, D_MODEL), lambda b, *_: (ctx_row, 0, chunk))


class Split(NamedTuple):
    start: int
    width: int
    dtype: object
    scale: float
    norm: bool
    rope: bool
    transposed: bool = False


def _group_rms_norm(p, gain, gmat):
    ss = p * p
    hi = ss.astype(BF16)
    lo = (ss - hi.astype(F32)).astype(BF16)
    blocks = []
    for j in range(p.shape[1] // MXU_DIM):
        sl = slice(j * MXU_DIM, (j + 1) * MXU_DIM)
        blocks.append(_dot(hi[:, sl], gmat) + _dot(lo[:, sl], gmat))
    ms = blocks[0] if len(blocks) == 1 else jnp.concatenate(blocks, axis=1)
    return p * lax.rsqrt(ms + EPS) * gain


def _rope(p, cos, sin):
    tm = p.shape[0]
    lane = lax.broadcasted_iota(jnp.int32, (tm, LANES), 1)
    first = (lane & 16) == 0
    outs = []
    for j in range(p.shape[1] // LANES):
        xb = p[:, j * LANES:(j + 1) * LANES]
        partner = jnp.where(first, pltpu.roll(xb, LANES - 16, 1), pltpu.roll(xb, 16, 1))
        outs.append(xb * cos + partner * sin)
    return jnp.concatenate(outs, axis=1)


def _inproj_kernel(x_ref, sc_ref, sh_ref, w_ref, *rest, splits, use_norm, use_rope):
    idx = 0
    if use_norm:
        gain_ref, gmat_ref = rest[0], rest[1]
        idx = 2
    if use_rope:
        cos_ref, sin_ref = rest[idx], rest[idx + 1]
        idx += 2
    out_refs = rest[idx:]
    sub = min(ROW_TILE, x_ref.shape[0])
    n_sub = x_ref.shape[0] // sub

    def project(r):
        rs = slice(r * sub, (r + 1) * sub)
        h = (x_ref[rs, :] * (1.0 + sc_ref[...]) + sh_ref[...]).astype(BF16)
        return [_dot(h, w_ref[:, sp.start:sp.start + sp.width]) for sp in splits]

    def finish(r, ps):
        rs = slice(r * sub, (r + 1) * sub)
        gain_off = 0
        for sp, p, o_ref in zip(splits, ps, out_refs):
            if sp.norm:
                p = _group_rms_norm(p, gain_ref[:, gain_off:gain_off + sp.width], gmat_ref[...])
                gain_off += sp.width
            if sp.rope and use_rope:
                p = _rope(p, cos_ref[rs, :], sin_ref[rs, :])
            if sp.scale != 1.0:
                p = p * sp.scale
            if sp.transposed:
                o_ref[:, rs] = p.T.astype(o_ref.dtype)
            else:
                o_ref[rs, :] = p.astype(o_ref.dtype)

    ps_next = project(0)
    for r in range(n_sub):
        ps = ps_next
        if r + 1 < n_sub:
            ps_next = project(r + 1)
        finish(r, ps)


def _inproj(x, mod, w, splits, *, tm, per_batch, ctx_row, gain=None, gmat=None, rope=None):
    bsz, rows, _ = x.shape
    n_cols = w.shape[1]
    use_norm = gain is not None
    use_rope = rope is not None
    in_specs = [
        pl.BlockSpec((None, tm, D_MODEL), lambda b, i: (b, i, 0)),
        _mod_spec(1, per_batch, ctx_row),
        _mod_spec(0, per_batch, ctx_row),
        pl.BlockSpec((D_MODEL, n_cols), lambda b, i: (0, 0)),
    ]
    args = [x, mod, mod, w]
    if use_norm:
        in_specs += [pl.BlockSpec(gain.shape, lambda b, i: (0, 0)),
                     pl.BlockSpec(gmat.shape, lambda b, i: (0, 0))]
        args += [gain, gmat]
    if use_rope:
        in_specs += [pl.BlockSpec((tm, LANES), lambda b, i: (i, 0))] * 2
        args += [rope[0], rope[1]]
    out_specs = [pl.BlockSpec((None, sp.width, tm), lambda b, i: (b, 0, i)) if sp.transposed
                 else pl.BlockSpec((None, tm, sp.width), lambda b, i: (b, i, 0)) for sp in splits]
    out_shape = [jax.ShapeDtypeStruct((bsz, sp.width, rows) if sp.transposed else (bsz, rows, sp.width), sp.dtype)
                 for sp in splits]
    return pl.pallas_call(
        functools.partial(_inproj_kernel, splits=tuple(splits), use_norm=use_norm, use_rope=use_rope),
        grid=(bsz, rows // tm),
        in_specs=in_specs,
        out_specs=out_specs,
        out_shape=out_shape,
        compiler_params=_params("parallel", "parallel"),
        name="inproj",
    )(*args)


def _head_qts(qp):
    qt = qp.astype(F32).T
    zeros = jnp.zeros((HEAD_DIM, qp.shape[0]), BF16)
    return [jnp.concatenate([qt[:HEAD_DIM].astype(BF16), zeros], axis=0),
            jnp.concatenate([zeros, qt[HEAD_DIM:].astype(BF16)], axis=0)]


def _softmax_pv_step(state, s, vt):
    cm = jnp.max(s, axis=0, keepdims=True)
    if state is None:
        return cm, _dot(vt, jnp.exp2(s - cm).astype(BF16))
    m, o = state
    m_new = jnp.maximum(m, cm)
    return m_new, jnp.exp2(m - m_new) * o + _dot(vt, jnp.exp2(s - m_new).astype(BF16))


def _attend(groups, k_refs, vt_refs, finish):
    chunks = [(k_ref, vt_ref, slice(c * KEY_CHUNK, (c + 1) * KEY_CHUNK))
              for k_ref, vt_ref in zip(k_refs, vt_refs) for c in range(k_ref.shape[0] // KEY_CHUNK)]
    items = [(g, j) for g in range(len(groups)) for j in range(len(chunks))]

    def scores(item):
        g, j = item
        k_ref, _, ks = chunks[j]
        kc = k_ref[ks, :]
        return [_dot(kc, qt) for qt, _ in groups[g]]

    ones = jnp.ones((BF16_SUBLANES, KEY_CHUNK), BF16)
    pending = [scores(item) for item in items[:QK_LOOKAHEAD]]
    for t, (g, j) in enumerate(items):
        s_cur = pending.pop(0)
        if t + QK_LOOKAHEAD < len(items):
            pending.append(scores(items[t + QK_LOOKAHEAD]))
        if j == 0:
            state = [None] * len(groups[g])
        _, vt_ref, ks = chunks[j]
        state = [_softmax_pv_step(st, s, jnp.concatenate([vt_ref[v_rows, ks], ones], axis=0))
                 for st, s, (_, v_rows) in zip(state, s_cur, groups[g])]
        if j == len(chunks) - 1:
            finish(g, [(o[:-BF16_SUBLANES], o[-BF16_SUBLANES:-BF16_SUBLANES + 1]) for _, o in state])


def _attn_gqa_kernel(q_ref, *refs, n_parts):
    k_refs = refs[:n_parts]
    vt_refs = refs[n_parts:2 * n_parts]
    o_ref = refs[2 * n_parts]
    groups, places = [], []
    for r in range(q_ref.shape[0] // Q_TILE):
        for pr in range(q_ref.shape[1] // LANES):
            place = (slice(r * Q_TILE, (r + 1) * Q_TILE), slice(pr * LANES, (pr + 1) * LANES))
            groups.append([(qt, slice(half * HEAD_DIM, (half + 1) * HEAD_DIM))
                           for half, qt in enumerate(_head_qts(q_ref[place]))])
            places.append(place)

    def finish(g, pair):
        o_pair = jnp.concatenate([o / l for o, l in pair], axis=0)
        o_ref[places[g]] = o_pair.T.astype(o_ref.dtype)

    _attend(groups, k_refs, vt_refs, finish)


def _attn_gqa(q, ks, vts, *, tq):
    bsz, rows, _ = q.shape
    n_parts = len(ks)
    qw = A_WIDTH // 2
    k_specs = [pl.BlockSpec((None, k.shape[1], LANES), lambda b, p, i: (b, 0, p)) for k in ks]
    vt_specs = [pl.BlockSpec((None, LANES, vt.shape[2]), lambda b, p, i: (b, p, 0)) for vt in vts]
    return pl.pallas_call(
        functools.partial(_attn_gqa_kernel, n_parts=n_parts),
        grid=(bsz, 2, rows // tq),
        in_specs=[pl.BlockSpec((None, tq, qw), lambda b, p, i: (b, i, p))] + k_specs + vt_specs,
        out_specs=pl.BlockSpec((None, tq, qw), lambda b, p, i: (b, i, p)),
        out_shape=jax.ShapeDtypeStruct((bsz, rows, A_WIDTH), BF16),
        compiler_params=_params("parallel", "parallel", "parallel"),
        name="attn_gqa",
    )(q, *ks, *vts)


def _attn_diff_kernel(lam_ref, gain_ref, q_ref, *refs, n_parts, lam_init):
    k_refs = refs[:n_parts]
    vt_refs = refs[n_parts:2 * n_parts]
    o_ref = refs[2 * n_parts]
    lv = lam_ref[...]
    lam = (jnp.exp(jnp.sum(lv[0:1] * lv[1:2], axis=1, keepdims=True))
           - jnp.exp(jnp.sum(lv[2:3] * lv[3:4], axis=1, keepdims=True)) + lam_init)
    rows = slice(0, C_V_DIM)
    groups = []
    for r in range(q_ref.shape[0] // Q_TILE):
        groups.append([(qt, rows) for qt in _head_qts(q_ref[r * Q_TILE:(r + 1) * Q_TILE, :])])

    def finish(g, comps):
        (o0, l0), (o1, l1) = comps
        o = o0 * (1.0 / l0) - o1 * (lam / l1)
        ms = jnp.mean(o * o, axis=0, keepdims=True)
        o = (o * lax.rsqrt(ms + EPS)).T
        o_ref[g * Q_TILE:(g + 1) * Q_TILE, :] = (o * gain_ref[...] * (1.0 - lam_init)).astype(o_ref.dtype)

    _attend(groups, k_refs, vt_refs, finish)


def _attn_diff(lam_vecs, gain, q, ks, vts, *, tq, lam_init):
    bsz, rows, _ = q.shape
    n_parts = len(ks)
    heads = C_WIDTH // C_V_DIM
    k_specs = [pl.BlockSpec((None, k.shape[1], LANES), lambda b, h, i: (b, 0, h)) for k in ks]
    vt_specs = [pl.BlockSpec((None, C_V_DIM, vt.shape[2]), lambda b, h, i: (b, h, 0)) for vt in vts]
    return pl.pallas_call(
        functools.partial(_attn_diff_kernel, n_parts=n_parts, lam_init=lam_init),
        grid=(bsz, heads, rows // tq),
        in_specs=[pl.BlockSpec(lam_vecs.shape, lambda b, h, i: (0, 0)),
                  pl.BlockSpec(gain.shape, lambda b, h, i: (0, 0)),
                  pl.BlockSpec((None, tq, LANES), lambda b, h, i: (b, i, h))] + k_specs + vt_specs,
        out_specs=pl.BlockSpec((None, tq, LANES), lambda b, h, i: (b, i, h)),
        out_shape=jax.ShapeDtypeStruct((bsz, rows, C_WIDTH), BF16),
        compiler_params=_params("parallel", "parallel", "parallel"),
        name="attn_diff",
    )(lam_vecs, gain, q, *ks, *vts)


def _pool_kernel(u_ref, w_ref, ps_ref, o_ref, sh_ref, *, chunk):
    t_len = u_ref.shape[0]
    padded = t_len + 2 * POOL_PAD
    zeros = jnp.zeros((POOL_PAD, B_WIDTH), F32)
    sh_ref[0, 0:POOL_PAD, :] = zeros
    sh_ref[0, POOL_PAD + t_len:padded, :] = zeros
    sh_ref[0, POOL_PAD:POOL_PAD + t_len, :] = u_ref[...]
    shifted_len = padded - SUBLANES
    per_col = LANES // B_GROUP_DIM
    col_windows = [POOL_WINDOWS[c * per_col:(c + 1) * per_col] for c in range(B_WIDTH // LANES)]

    def offsets(w):
        return range(-(w // 2), w - w // 2)

    for col, ws in enumerate(col_windows):
        cs = slice(col * LANES, (col + 1) * LANES)
        for r in sorted({(POOL_PAD + j) % SUBLANES for j in offsets(ws[-1])} - {0}):
            for c0 in range(0, shifted_len, chunk):
                n = min(chunk, shifted_len - c0)
                sh_ref[r, c0:c0 + n, cs] = sh_ref[0, c0 + r:c0 + r + n, cs]

    grp = lax.broadcasted_iota(jnp.int32, (chunk, LANES), 1) // B_GROUP_DIM
    for c in range(t_len // chunk):
        r0 = c * chunk
        t = lax.broadcasted_iota(jnp.int32, (chunk, LANES), 0) + r0
        cols = []
        for col, ws in enumerate(col_windows):
            cs = slice(col * LANES, (col + 1) * LANES)

            def win(j):
                off = POOL_PAD + j
                lo = r0 + off // SUBLANES * SUBLANES
                return sh_ref[off % SUBLANES, lo:lo + chunk, cs]

            acc, done, s, half, tail = None, [], None, None, None
            for k, w in enumerate(ws):
                new = [win(j) for j in offsets(w) if j not in done]
                done += list(offsets(w))
                acc = functools.reduce(jnp.add, ([] if acc is None else [acc]) + new)
                s = acc if s is None else jnp.where(grp == k, acc, s)
                half = w // 2 if half is None else jnp.where(grp == k, w // 2, half)
                tail = w - 1 - w // 2 if tail is None else jnp.where(grp == k, w - 1 - w // 2, tail)
            cnt = (jnp.minimum(t + tail + 1, t_len) - jnp.maximum(t - half, 0)).astype(F32)
            cols.append(s / cnt - win(0))
        pooled = jnp.concatenate(cols, axis=1)
        mixed = _dot(pooled.astype(BF16), w_ref[...]) * ps_ref[...]
        o_ref[r0:r0 + chunk, :] = mixed.astype(o_ref.dtype)


def _pool(u, w_bd, pool_scale):
    bsz, t_len, _ = u.shape
    return pl.pallas_call(
        functools.partial(_pool_kernel, chunk=min(t_len, 256)),
        grid=(bsz,),
        in_specs=[pl.BlockSpec((None, t_len, B_WIDTH), lambda b: (b, 0, 0)),
                  pl.BlockSpec(w_bd.shape, lambda b: (0, 0)),
                  pl.BlockSpec(pool_scale.shape, lambda b: (0, 0))],
        out_specs=pl.BlockSpec((None, t_len, B_WIDTH), lambda b: (b, 0, 0)),
        out_shape=jax.ShapeDtypeStruct((bsz, t_len, B_WIDTH), BF16),
        scratch_shapes=[pltpu.VMEM((SUBLANES, t_len + 2 * POOL_PAD, B_WIDTH), F32)],
        compiler_params=_params("parallel"),
        name="pool_mixer",
    )(u, w_bd, pool_scale)


def _conv_kernel(u_ref, w_ref, b_ref, g_ref, beta_ref, o_ref, sh_ref, *, chunk):
    t_len = u_ref.shape[0]
    padded = t_len + 2 * CONV_PAD
    zeros = jnp.zeros((CONV_PAD, D_CH), F32)
    sh_ref[0, 0:CONV_PAD, :] = zeros
    sh_ref[0, CONV_PAD + t_len:padded, :] = zeros
    for c in range(t_len // chunk):
        rs = slice(c * chunk, (c + 1) * chunk)
        sh_ref[0, CONV_PAD + c * chunk:CONV_PAD + (c + 1) * chunk, :] = (
            u_ref[rs, 0:D_CH] * jax.nn.sigmoid(u_ref[rs, D_CH:2 * D_CH]))
    shifted_len = padded - SUBLANES
    for r in range(1, SUBLANES):
        for c0 in range(0, shifted_len, chunk):
            n = min(chunk, shifted_len - c0)
            sh_ref[r, c0:c0 + n, :] = sh_ref[0, c0 + r:c0 + r + n, :]
    first = CONV_PAD - CONV_WIDTH // 2
    for c in range(t_len // chunk):
        acc = None
        for j in range(CONV_WIDTH):
            off = first + j
            lo = c * chunk + off // SUBLANES * SUBLANES
            term = sh_ref[off % SUBLANES, lo:lo + chunk, :] * w_ref[j:j + 1, :]
            acc = term if acc is None else acc + term
        z = _layer_norm(acc + b_ref[...], g_ref[...], beta_ref[...])
        o_ref[c * chunk:(c + 1) * chunk, :] = jax.nn.silu(z).astype(o_ref.dtype)


def _conformer_conv(u, conv_w, conv_b, ln_g, ln_b):
    bsz, t_len, _ = u.shape
    vec = pl.BlockSpec((1, D_CH), lambda b: (0, 0))
    return pl.pallas_call(
        functools.partial(_conv_kernel, chunk=128),
        grid=(bsz,),
        in_specs=[pl.BlockSpec((None, t_len, 2 * D_CH), lambda b: (b, 0, 0)),
                  pl.BlockSpec(conv_w.shape, lambda b: (0, 0)), vec, vec, vec],
        out_specs=pl.BlockSpec((None, t_len, D_CH), lambda b: (b, 0, 0)),
        out_shape=jax.ShapeDtypeStruct((bsz, t_len, D_CH), BF16),
        scratch_shapes=[pltpu.VMEM((SUBLANES, t_len + 2 * CONV_PAD, D_CH), F32)],
        compiler_params=_params("parallel"),
        name="conformer_conv",
    )(u, conv_w, conv_b, ln_g, ln_b)


def _tail_kernel(a_ref, m_ref, wp_ref, x_ref, gate1_ref, g1_ref, b1_ref, sc_ref, sh_ref, gate_ref,
                 wi_ref, wo_ref, g_ref, b_ref, o_ref, x1_ref, acc_ref):
    wa = a_ref.shape[1]
    sub = min(ROW_TILE, x_ref.shape[0])
    n_sub = x_ref.shape[0] // sub

    def project(r):
        rs = slice(r * sub, (r + 1) * sub)
        return _dot(a_ref[rs, :], wp_ref[0:wa, :]) + _dot(m_ref[rs, :], wp_ref[wa:, :])

    y_next = project(0)
    for r in range(n_sub):
        y = y_next
        if r + 1 < n_sub:
            y_next = project(r + 1)
        rs = slice(r * sub, (r + 1) * sub)
        z = ALPHA * x_ref[rs, :] + gate1_ref[...] * y
        x1_ref[rs, :] = _layer_norm(z, g1_ref[...], b1_ref[...])

    h = (x1_ref[...] * (1.0 + sc_ref[...]) + sh_ref[...]).astype(BF16)
    n_chunks = FFN_HIDDEN // FFN_CHUNK

    def up(c):
        lo = c * FFN_CHUNK
        return (_dot(h, wi_ref[:, lo:lo + FFN_CHUNK]),
                _dot(h, wi_ref[:, FFN_HIDDEN + lo:FFN_HIDDEN + lo + FFN_CHUNK]))

    pending = [up(c) for c in range(FFN_LOOKAHEAD)]
    for c in range(n_chunks):
        a, g = pending.pop(0)
        if c + FFN_LOOKAHEAD < n_chunks:
            pending.append(up(c + FFN_LOOKAHEAD))
        y = _dot((jax.nn.silu(g) * a).astype(BF16), wo_ref[c * FFN_CHUNK:(c + 1) * FFN_CHUNK, :])
        if c == 0:
            acc_ref[...] = y
        else:
            acc_ref[...] += y
    z = ALPHA * x1_ref[...] + gate_ref[...] * acc_ref[...]
    o_ref[...] = _layer_norm(z, g_ref[...], b_ref[...])


def _tail(attn, mix, w_proj, x, mod, ln1_g, ln1_b, w_in, w_out, layer, ln2_g, ln2_b, *, tm, per_batch, ctx_row):
    bsz, rows, _ = x.shape
    vec = pl.BlockSpec((1, D_MODEL), lambda b, i: (0, 0))
    resident = dict(pipeline_mode=pl.Buffered(1))
    return pl.pallas_call(
        _tail_kernel,
        grid=(bsz, rows // tm),
        in_specs=[pl.BlockSpec((None, tm, attn.shape[2]), lambda b, i: (b, i, 0)),
                  pl.BlockSpec((None, tm, mix.shape[2]), lambda b, i: (b, i, 0)),
                  pl.BlockSpec(w_proj.shape, lambda b, i: (0, 0), **resident),
                  pl.BlockSpec((None, tm, D_MODEL), lambda b, i: (b, i, 0)),
                  _mod_spec(2, per_batch, ctx_row),
                  vec, vec,
                  _mod_spec(4, per_batch, ctx_row),
                  _mod_spec(3, per_batch, ctx_row),
                  _mod_spec(5, per_batch, ctx_row),
                  pl.BlockSpec((None,) + w_in.shape[1:], lambda b, i: (layer, 0, 0), **resident),
                  pl.BlockSpec((None,) + w_out.shape[1:], lambda b, i: (layer, 0, 0), **resident),
                  vec, vec],
        out_specs=pl.BlockSpec((None, tm, D_MODEL), lambda b, i: (b, i, 0)),
        out_shape=jax.ShapeDtypeStruct(x.shape, F32),
        scratch_shapes=[pltpu.VMEM((tm, D_MODEL), F32), pltpu.VMEM((tm, D_MODEL), F32)],
        compiler_params=_params("parallel", "parallel"),
        name="outproj_ffn",
    )(attn, mix, w_proj, x, mod, ln1_g, ln1_b, mod, mod, mod, w_in, w_out, ln2_g, ln2_b)


def _rope_tables(n_tokens):
    rows = n_tokens // GRID_W
    row = jnp.repeat(jnp.arange(rows, dtype=F32), GRID_W)
    col = jnp.tile(jnp.arange(GRID_W, dtype=F32), rows)
    axis_dim = HEAD_DIM // 2
    freqs = ROPE_THETA ** (-jnp.arange(0, axis_dim, 2, dtype=F32) / axis_dim)
    ang_r, ang_c = row[:, None] * freqs, col[:, None] * freqs
    cos = jnp.concatenate([jnp.cos(ang_r)] * 2 + [jnp.cos(ang_c)] * 2, axis=-1)
    sin = jnp.concatenate([-jnp.sin(ang_r), jnp.sin(ang_r), -jnp.sin(ang_c), jnp.sin(ang_c)], axis=-1)
    return jnp.tile(cos, (1, 2)), jnp.tile(sin, (1, 2))


def _pair_heads_cols(w):
    return w.reshape(w.shape[0], 2, 2, 3, HEAD_DIM).transpose(0, 1, 3, 2, 4).reshape(w.shape[0], A_WIDTH)


def _pair_heads_rows(w):
    return w.reshape(2, 2, 3, HEAD_DIM, w.shape[1]).transpose(0, 2, 1, 3, 4).reshape(A_WIDTH, w.shape[1])


def _block_diag(w_pool):
    out = jnp.zeros((B_WIDTH, B_WIDTH), w_pool.dtype)
    for g in range(B_GROUPS):
        sl = slice(g * B_GROUP_DIM, (g + 1) * B_GROUP_DIM)
        out = out.at[sl, sl].set(w_pool[g])
    return out


def kernel(x, c, ctx, c_ctx, ab_w_in, ab_q_gain, ab_k_gain, ab_w_pool, ab_pool_scale, ab_w_out,
           cd_w_in, cd_lambda_q1, cd_lambda_k1, cd_lambda_q2, cd_lambda_k2, cd_subln_gain,
           cd_conv_w, cd_conv_b, cd_conv_ln_g, cd_conv_ln_b, cd_w_out,
           ada_w, ada_b, ln1_g, ln1_b, ln2_g, ln2_b, ffn_w_in, ffn_w_out):
    bsz, t_len, _ = x.shape
    n_ctx = ctx.shape[1]
    ctx_row = bsz
    mod_rows = -(-(bsz + 1) // 8) * 8
    cc = jnp.zeros((mod_rows, D_MODEL), F32).at[:bsz].set(c).at[bsz].set(c_ctx)
    mods = _modulation(cc, ada_w, ada_b)
    rope = _rope_tables(t_len)
    row = lambda v: v.reshape(1, -1)
    head_of = jnp.arange(MXU_DIM) // HEAD_DIM
    gmat = jnp.where(head_of[:, None] == head_of[None, :], 1.0 / HEAD_DIM, 0.0).astype(BF16)
    w_ffn_in, w_ffn_out = ffn_w_in.astype(BF16), ffn_w_out.astype(BF16)

    l, i = 0, 0
    v0 = A_WIDTH + A_KV_WIDTH
    w_raw, wo_raw = ab_w_in[i].astype(BF16), ab_w_out[i].astype(BF16)
    w_in = jnp.concatenate([_pair_heads_cols(w_raw[:, :A_WIDTH]), w_raw[:, A_WIDTH:]], axis=1)
    w_out = jnp.concatenate([_pair_heads_rows(wo_raw[:A_WIDTH]), wo_raw[A_WIDTH:]], axis=0)
    gain = jnp.concatenate([jnp.tile(ab_q_gain[i], A_WIDTH // HEAD_DIM),
                            jnp.tile(ab_k_gain[i], A_KV_WIDTH // HEAD_DIM)]).reshape(1, -1)
    splits = [Split(0, A_WIDTH, BF16, Q_SCALE, True, True),
              Split(A_WIDTH, A_KV_WIDTH, BF16, 1.0, True, True),
              Split(v0 + A_KV_WIDTH, B_WIDTH, F32, 1.0, False, False),
              Split(v0, A_KV_WIDTH, BF16, 1.0, False, False, transposed=True)]
    w_bd = _block_diag(ab_w_pool[i]).astype(BF16)
    pool_scale = row(ab_pool_scale[i])
    g1, b1, g2, b2 = row(ln1_g[l]), row(ln1_b[l]), row(ln2_g[l]), row(ln2_b[l])

    q, k, u, vt = _inproj(x, mods[l], w_in, splits, tm=PROJ_TM, per_batch=True, ctx_row=ctx_row,
                          gain=gain, gmat=gmat, rope=rope)
    qc, kc, uc, vtc = _inproj(ctx, mods[l], w_in, splits, tm=n_ctx, per_batch=False, ctx_row=ctx_row,
                              gain=gain, gmat=gmat)
    o = _attn_gqa(q, [kc, k], [vtc, vt], tq=GQA_TQ)
    oc = _attn_gqa(qc, [kc], [vtc], tq=n_ctx)
    mix = _pool(u, w_bd, pool_scale)
    mixc = _pool(uc, w_bd, pool_scale)
    x = _tail(o, mix, w_out, x, mods[l], g1, b1, w_ffn_in, w_ffn_out, l, g2, b2,
              tm=FFN_TM, per_batch=True, ctx_row=ctx_row)
    flat = lambda t: t.reshape(1, bsz * n_ctx, t.shape[-1])
    xc = _tail(flat(oc), flat(mixc), w_out, flat(ctx), mods[l], g1, b1, w_ffn_in, w_ffn_out, l, g2, b2,
               tm=FFN_TM, per_batch=False, ctx_row=ctx_row).reshape(bsz, n_ctx, D_MODEL)

    l, i = 1, 0
    lam_init = 0.8 - 0.6 * math.exp(-0.3 * l)
    w_in = cd_w_in[i].astype(BF16)
    w_out = cd_w_out[i].astype(BF16)
    v_split = Split(2 * C_WIDTH, C_WIDTH, BF16, 1.0, False, False, transposed=True)
    splits = [Split(0, C_WIDTH, BF16, Q_SCALE, False, True),
              Split(C_WIDTH, C_WIDTH, BF16, 1.0, False, True),
              Split(3 * C_WIDTH, 2 * D_CH, F32, 1.0, False, False),
              v_split]
    ctx_splits = [Split(C_WIDTH, C_WIDTH, BF16, 1.0, False, False), v_split]
    lam_vecs = jnp.stack([cd_lambda_q1[i], cd_lambda_k1[i], cd_lambda_q2[i], cd_lambda_k2[i]])
    sub_gain = row(cd_subln_gain[i])
    g1, b1, g2, b2 = row(ln1_g[l]), row(ln1_b[l]), row(ln2_g[l]), row(ln2_b[l])

    q, k, u, vt = _inproj(x, mods[l], w_in, splits, tm=PROJ_TM, per_batch=True, ctx_row=ctx_row, rope=rope)
    kc, vtc = _inproj(xc, mods[l], w_in, ctx_splits, tm=n_ctx, per_batch=False, ctx_row=ctx_row)
    o = _attn_diff(lam_vecs, sub_gain, q, [kc, k], [vtc, vt], tq=DIFF_TQ, lam_init=lam_init)
    conv = _conformer_conv(u, cd_conv_w[i], row(cd_conv_b[i]), row(cd_conv_ln_g[i]), row(cd_conv_ln_b[i]))
    return _tail(o, conv, w_out, x, mods[l], g1, b1, w_ffn_in, w_ffn_out, l, g2, b2,
                 tm=FFN_TM, per_batch=True, ctx_row=ctx_row)
```

```python
import functools
import math
from typing import NamedTuple

import jax
import jax.numpy as jnp
from jax import lax
from jax.experimental import pallas as pl
from jax.experimental.pallas import tpu as pltpu

F32 = jnp.float32
BF16 = jnp.bfloat16

D_MODEL = 1024
DEPTH = 2
GRID_W = 64
HEAD_DIM = 64
ROPE_THETA = 10000.0
EPS = 1e-6

B_WIDTH = D_MODEL // 4
B_GROUPS = 4
B_GROUP_DIM = B_WIDTH // B_GROUPS
POOL_WINDOWS = (2, 4, 8, 16)
A_WIDTH = D_MODEL - B_WIDTH
A_KV_WIDTH = 4 * HEAD_DIM
D_CH = D_MODEL // 4
C_WIDTH = D_MODEL - D_CH
C_V_DIM = 2 * HEAD_DIM
CONV_WIDTH = 31
FFN_HIDDEN = -(-8 * D_MODEL // (3 * 256)) * 256
ALPHA = (2.0 * DEPTH) ** 0.25

LANES = 128
SUBLANES = 8
BF16_SUBLANES = 16
Q_SCALE = HEAD_DIM ** -0.5 * math.log2(math.e)
MXU_DIM = 256
VMEM_LIMIT = 48 * 1024 * 1024

POOL_PAD = 8
CONV_PAD = 16
FFN_CHUNK = 256
FFN_LOOKAHEAD = 1
KEY_CHUNK = MXU_DIM
QK_LOOKAHEAD = 3
ROW_TILE = MXU_DIM
PROJ_TM = 1024
FFN_TM = 512
TAIL_TM = 1024
Q_TILE = MXU_DIM
GQA_TQ = 1024
DIFF_TQ = 2048


def _params(*semantics):
    return pltpu.CompilerParams(dimension_semantics=semantics, vmem_limit_bytes=VMEM_LIMIT)


def _layer_norm(z, g, b):
    mu = jnp.mean(z, axis=-1, keepdims=True)
    zc = z - mu
    var = jnp.mean(zc * zc, axis=-1, keepdims=True)
    return zc * lax.rsqrt(var + EPS) * g + b


def _dot(a, b):
    return jnp.dot(a, b, preferred_element_type=F32)


def _dot_nt(a, b):
    return lax.dot_general(a, b, (((1,), (1,)), ((), ())), preferred_element_type=F32)


def _mod_kernel(c_ref, w_ref, b_ref, o_ref):
    s = jax.nn.silu(c_ref[...])
    w = w_ref[...]
    s_hi, w_hi = s.astype(BF16), w.astype(BF16)
    s_lo = (s - s_hi.astype(F32)).astype(BF16)
    w_lo = (w - w_hi.astype(F32)).astype(BF16)
    o_ref[...] = _dot(s_hi, w_hi) + _dot(s_lo, w_hi) + _dot(s_hi, w_lo) + b_ref[...]


def _modulation(cc, ada_w, ada_b):
    rows = cc.shape[0]
    tn = 1536
    out = pl.pallas_call(
        _mod_kernel,
        grid=(DEPTH, 6 * D_MODEL // tn),
        in_specs=[
            pl.BlockSpec((rows, D_MODEL), lambda l, j: (0, 0)),
            pl.BlockSpec((None, D_MODEL, tn), lambda l, j: (l, 0, j)),
            pl.BlockSpec((None, 1, tn), lambda l, j: (l, 0, j)),
        ],
        out_specs=pl.BlockSpec((None, rows, tn), lambda l, j: (l, 0, j)),
        out_shape=jax.ShapeDtypeStruct((DEPTH, rows, 6 * D_MODEL), F32),
        compiler_params=_params("parallel", "parallel"),
        name="adaln_mod",
    )(cc, ada_w, ada_b.reshape(DEPTH, 1, 6 * D_MODEL))
    return out.reshape(DEPTH, rows, 1, 6 * D_MODEL)


def _mod_spec(chunk, per_batch, ctx_row):
    if per_batch:
        return pl.BlockSpec((None, 1, D_MODEL), lambda b, *_: (b, 0, chunk))
    return pl.BlockSpec((None, 1, D_MODEL), lambda b, *_: (ctx_row, 0, chunk))


class Split(NamedTuple):
    start: int
    width: int
    dtype: object
    scale: float
    norm: bool
    rope: bool
    transposed: bool = False


def _group_rms_norm(p, gain, gmat):
    ss = p * p
    hi = ss.astype(BF16)
    lo = (ss - hi.astype(F32)).astype(BF16)
    blocks = []
    for j in range(p.shape[1] // MXU_DIM):
        sl = slice(j * MXU_DIM, (j + 1) * MXU_DIM)
        blocks.append(_dot(hi[:, sl], gmat) + _dot(lo[:, sl], gmat))
    ms = blocks[0] if len(blocks) == 1 else jnp.concatenate(blocks, axis=1)
    return p * lax.rsqrt(ms + EPS) * gain


def _rope(p, cos, sin):
    tm = p.shape[0]
    lane = lax.broadcasted_iota(jnp.int32, (tm, LANES), 1)
    first = (lane & 16) == 0
    outs = []
    for j in range(p.shape[1] // LANES):
        xb = p[:, j * LANES:(j + 1) * LANES]
        partner = jnp.where(first, pltpu.roll(xb, LANES - 16, 1), pltpu.roll(xb, 16, 1))
        outs.append(xb * cos + partner * sin)
    return jnp.concatenate(outs, axis=1)


def _inproj_kernel(x_ref, sc_ref, sh_ref, w_ref, *rest, splits, use_norm, use_rope):
    idx = 0
    if use_norm:
        gain_ref, gmat_ref = rest[0], rest[1]
        idx = 2
    if use_rope:
        cos_ref, sin_ref = rest[idx], rest[idx + 1]
        idx += 2
    out_refs = rest[idx:]
    sub = min(ROW_TILE, x_ref.shape[0])
    n_sub = x_ref.shape[0] // sub

    def project(r):
        rs = slice(r * sub, (r + 1) * sub)
        h = (x_ref[rs, :] * (1.0 + sc_ref[...]) + sh_ref[...]).astype(BF16)
        return [_dot(h, w_ref[:, sp.start:sp.start + sp.width]) for sp in splits]

    def finish(r, ps):
        rs = slice(r * sub, (r + 1) * sub)
        gain_off = 0
        for sp, p, o_ref in zip(splits, ps, out_refs):
            if sp.norm:
                p = _group_rms_norm(p, gain_ref[:, gain_off:gain_off + sp.width], gmat_ref[...])
                gain_off += sp.width
            if sp.rope and use_rope:
                p = _rope(p, cos_ref[rs, :], sin_ref[rs, :])
            if sp.scale != 1.0:
                p = p * sp.scale
            if sp.transposed:
                o_ref[:, rs] = p.T.astype(o_ref.dtype)
            else:
                o_ref[rs, :] = p.astype(o_ref.dtype)

    ps_next = project(0)
    for r in range(n_sub):
        ps = ps_next
        if r + 1 < n_sub:
            ps_next = project(r + 1)
        finish(r, ps)


def _inproj(x, mod, w, splits, *, tm, per_batch, ctx_row, gain=None, gmat=None, rope=None):
    bsz, rows, _ = x.shape
    n_cols = w.shape[1]
    use_norm = gain is not None
    use_rope = rope is not None
    in_specs = [
        pl.BlockSpec((None, tm, D_MODEL), lambda b, i: (b, i, 0)),
        _mod_spec(1, per_batch, ctx_row),
        _mod_spec(0, per_batch, ctx_row),
        pl.BlockSpec((D_MODEL, n_cols), lambda b, i: (0, 0)),
    ]
    args = [x, mod, mod, w]
    if use_norm:
        in_specs += [pl.BlockSpec(gain.shape, lambda b, i: (0, 0)),
                     pl.BlockSpec(gmat.shape, lambda b, i: (0, 0))]
        args += [gain, gmat]
    if use_rope:
        in_specs += [pl.BlockSpec((tm, LANES), lambda b, i: (i, 0))] * 2
        args += [rope[0], rope[1]]
    out_specs = [pl.BlockSpec((None, sp.width, tm), lambda b, i: (b, 0, i)) if sp.transposed
                 else pl.BlockSpec((None, tm, sp.width), lambda b, i: (b, i, 0)) for sp in splits]
    out_shape = [jax.ShapeDtypeStruct((bsz, sp.width, rows) if sp.transposed else (bsz, rows, sp.width), sp.dtype)
                 for sp in splits]
    return pl.pallas_call(
        functools.partial(_inproj_kernel, splits=tuple(splits), use_norm=use_norm, use_rope=use_rope),
        grid=(bsz, rows // tm),
        in_specs=in_specs,
        out_specs=out_specs,
        out_shape=out_shape,
        compiler_params=_params("parallel", "parallel"),
        name="inproj",
    )(*args)


def _head_qts(qp):
    qt = qp.astype(F32).T
    zeros = jnp.zeros((HEAD_DIM, qp.shape[0]), BF16)
    return [jnp.concatenate([qt[:HEAD_DIM].astype(BF16), zeros], axis=0),
            jnp.concatenate([zeros, qt[HEAD_DIM:].astype(BF16)], axis=0)]


def _softmax_pv_step(state, s, vt):
    cm = jnp.max(s, axis=0, keepdims=True)
    if state is None:
        return cm, _dot(vt, jnp.exp2(s - cm).astype(BF16))
    m, o = state
    m_new = jnp.maximum(m, cm)
    return m_new, jnp.exp2(m - m_new) * o + _dot(vt, jnp.exp2(s - m_new).astype(BF16))


def _attend(groups, k_refs, vt_refs, finish):
    chunks = [(k_ref, vt_ref, slice(c * KEY_CHUNK, (c + 1) * KEY_CHUNK))
              for k_ref, vt_ref in zip(k_refs, vt_refs) for c in range(k_ref.shape[0] // KEY_CHUNK)]
    items = [(g, j) for g in range(len(groups)) for j in range(len(chunks))]

    def scores(item):
        g, j = item
        k_ref, _, ks = chunks[j]
        kc = k_ref[ks, :]
        return [_dot(kc, qt) for qt, _ in groups[g]]

    ones = jnp.ones((BF16_SUBLANES, KEY_CHUNK), BF16)
    pending = [scores(item) for item in items[:QK_LOOKAHEAD]]
    for t, (g, j) in enumerate(items):
        s_cur = pending.pop(0)
        if t + QK_LOOKAHEAD < len(items):
            pending.append(scores(items[t + QK_LOOKAHEAD]))
        if j == 0:
            state = [None] * len(groups[g])
        _, vt_ref, ks = chunks[j]
        state = [_softmax_pv_step(st, s, jnp.concatenate([vt_ref[v_rows, ks], ones], axis=0))
                 for st, s, (_, v_rows) in zip(state, s_cur, groups[g])]
        if j == len(chunks) - 1:
            finish(g, [(o[:-BF16_SUBLANES], o[-BF16_SUBLANES:-BF16_SUBLANES + 1]) for _, o in state])


def _attn_gqa_kernel(q_ref, *refs, n_parts):
    k_refs = refs[:n_parts]
    vt_refs = refs[n_parts:2 * n_parts]
    o_ref = refs[2 * n_parts]
    groups, places = [], []
    for r in range(q_ref.shape[0] // Q_TILE):
        for pr in range(q_ref.shape[1] // LANES):
            place = (slice(r * Q_TILE, (r + 1) * Q_TILE), slice(pr * LANES, (pr + 1) * LANES))
            groups.append([(qt, slice(half * HEAD_DIM, (half + 1) * HEAD_DIM))
                           for half, qt in enumerate(_head_qts(q_ref[place]))])
            places.append(place)

    def finish(g, pair):
        o_pair = jnp.concatenate([o / l for o, l in pair], axis=0)
        o_ref[places[g]] = o_pair.T.astype(o_ref.dtype)

    _attend(groups, k_refs, vt_refs, finish)


def _attn_gqa(q, ks, vts, *, tq):
    bsz, rows, _ = q.shape
    n_parts = len(ks)
    qw = A_WIDTH // 2
    k_specs = [pl.BlockSpec((None, k.shape[1], LANES), lambda b, p, i: (b, 0, p)) for k in ks]
    vt_specs = [pl.BlockSpec((None, LANES, vt.shape[2]), lambda b, p, i: (b, p, 0)) for vt in vts]
    return pl.pallas_call(
        functools.partial(_attn_gqa_kernel, n_parts=n_parts),
        grid=(bsz, 2, rows // tq),
        in_specs=[pl.BlockSpec((None, tq, qw), lambda b, p, i: (b, i, p))] + k_specs + vt_specs,
        out_specs=pl.BlockSpec((None, tq, qw), lambda b, p, i: (b, i, p)),
        out_shape=jax.ShapeDtypeStruct((bsz, rows, A_WIDTH), BF16),
        compiler_params=_params("parallel", "parallel", "parallel"),
        name="attn_gqa",
    )(q, *ks, *vts)


def _attn_diff_kernel(lam_ref, gain_ref, q_ref, *refs, n_parts, lam_init):
    k_refs = refs[:n_parts]
    vt_refs = refs[n_parts:2 * n_parts]
    o_ref = refs[2 * n_parts]
    lv = lam_ref[...]
    lam = (jnp.exp(jnp.sum(lv[0:1] * lv[1:2], axis=1, keepdims=True))
           - jnp.exp(jnp.sum(lv[2:3] * lv[3:4], axis=1, keepdims=True)) + lam_init)
    rows = slice(0, C_V_DIM)
    groups = []
    for r in range(q_ref.shape[0] // Q_TILE):
        groups.append([(qt, rows) for qt in _head_qts(q_ref[r * Q_TILE:(r + 1) * Q_TILE, :])])

    def finish(g, comps):
        (o0, l0), (o1, l1) = comps
        o = o0 * (1.0 / l0) - o1 * (lam / l1)
        ms = jnp.mean(o * o, axis=0, keepdims=True)
        o = (o * lax.rsqrt(ms + EPS)).T
        o_ref[g * Q_TILE:(g + 1) * Q_TILE, :] = (o * gain_ref[...] * (1.0 - lam_init)).astype(o_ref.dtype)

    _attend(groups, k_refs, vt_refs, finish)


def _attn_diff(lam_vecs, gain, q, ks, vts, *, tq, lam_init):
    bsz, rows, _ = q.shape
    n_parts = len(ks)
    heads = C_WIDTH // C_V_DIM
    k_specs = [pl.BlockSpec((None, k.shape[1], LANES), lambda b, h, i: (b, 0, h)) for k in ks]
    vt_specs = [pl.BlockSpec((None, C_V_DIM, vt.shape[2]), lambda b, h, i: (b, h, 0)) for vt in vts]
    return pl.pallas_call(
        functools.partial(_attn_diff_kernel, n_parts=n_parts, lam_init=lam_init),
        grid=(bsz, heads, rows // tq),
        in_specs=[pl.BlockSpec(lam_vecs.shape, lambda b, h, i: (0, 0)),
                  pl.BlockSpec(gain.shape, lambda b, h, i: (0, 0)),
                  pl.BlockSpec((None, tq, LANES), lambda b, h, i: (b, i, h))] + k_specs + vt_specs,
        out_specs=pl.BlockSpec((None, tq, LANES), lambda b, h, i: (b, i, h)),
        out_shape=jax.ShapeDtypeStruct((bsz, rows, C_WIDTH), BF16),
        compiler_params=_params("parallel", "parallel", "parallel"),
        name="attn_diff",
    )(lam_vecs, gain, q, *ks, *vts)


def _pool_kernel(u_ref, w_ref, ps_ref, o_ref, sh_ref, *, chunk):
    t_len = u_ref.shape[0]
    padded = t_len + 2 * POOL_PAD
    zeros = jnp.zeros((POOL_PAD, B_WIDTH), F32)
    sh_ref[0, 0:POOL_PAD, :] = zeros
    sh_ref[0, POOL_PAD + t_len:padded, :] = zeros
    sh_ref[0, POOL_PAD:POOL_PAD + t_len, :] = u_ref[...]
    shifted_len = padded - SUBLANES
    per_col = LANES // B_GROUP_DIM
    col_windows = [POOL_WINDOWS[c * per_col:(c + 1) * per_col] for c in range(B_WIDTH // LANES)]

    def offsets(w):
        return range(-(w // 2), w - w // 2)

    for col, ws in enumerate(col_windows):
        cs = slice(col * LANES, (col + 1) * LANES)
        for r in sorted({(POOL_PAD + j) % SUBLANES for j in offsets(ws[-1])} - {0}):
            for c0 in range(0, shifted_len, chunk):
                n = min(chunk, shifted_len - c0)
                sh_ref[r, c0:c0 + n, cs] = sh_ref[0, c0 + r:c0 + r + n, cs]

    grp = lax.broadcasted_iota(jnp.int32, (chunk, LANES), 1) // B_GROUP_DIM
    for c in range(t_len // chunk):
        r0 = c * chunk
        t = lax.broadcasted_iota(jnp.int32, (chunk, LANES), 0) + r0
        cols = []
        for col, ws in enumerate(col_windows):
            cs = slice(col * LANES, (col + 1) * LANES)

            def win(j):
                off = POOL_PAD + j
                lo = r0 + off // SUBLANES * SUBLANES
                return sh_ref[off % SUBLANES, lo:lo + chunk, cs]

            acc, done, s, half, tail = None, [], None, None, None
            for k, w in enumerate(ws):
                new = [win(j) for j in offsets(w) if j not in done]
                done += list(offsets(w))
                acc = functools.reduce(jnp.add, ([] if acc is None else [acc]) + new)
                s = acc if s is None else jnp.where(grp == k, acc, s)
                half = w // 2 if half is None else jnp.where(grp == k, w // 2, half)
                tail = w - 1 - w // 2 if tail is None else jnp.where(grp == k, w - 1 - w // 2, tail)
            cnt = (jnp.minimum(t + tail + 1, t_len) - jnp.maximum(t - half, 0)).astype(F32)
            cols.append(s / cnt - win(0))
        pooled = jnp.concatenate(cols, axis=1)
        mixed = _dot(pooled.astype(BF16), w_ref[...]) * ps_ref[...]
        o_ref[r0:r0 + chunk, :] = mixed.astype(o_ref.dtype)


def _pool(u, w_bd, pool_scale):
    bsz, t_len, _ = u.shape
    return pl.pallas_call(
        functools.partial(_pool_kernel, chunk=min(t_len, 256)),
        grid=(bsz,),
        in_specs=[pl.BlockSpec((None, t_len, B_WIDTH), lambda b: (b, 0, 0)),
                  pl.BlockSpec(w_bd.shape, lambda b: (0, 0)),
                  pl.BlockSpec(pool_scale.shape, lambda b: (0, 0))],
        out_specs=pl.BlockSpec((None, t_len, B_WIDTH), lambda b: (b, 0, 0)),
        out_shape=jax.ShapeDtypeStruct((bsz, t_len, B_WIDTH), BF16),
        scratch_shapes=[pltpu.VMEM((SUBLANES, t_len + 2 * POOL_PAD, B_WIDTH), F32)],
        compiler_params=_params("parallel"),
        name="pool_mixer",
    )(u, w_bd, pool_scale)


def _conv_kernel(u_ref, w_ref, b_ref, g_ref, beta_ref, o_ref, sh_ref, *, chunk):
    t_len = u_ref.shape[0]
    padded = t_len + 2 * CONV_PAD
    zeros = jnp.zeros((CONV_PAD, D_CH), F32)
    sh_ref[0, 0:CONV_PAD, :] = zeros
    sh_ref[0, CONV_PAD + t_len:padded, :] = zeros
    for c in range(t_len // chunk):
        rs = slice(c * chunk, (c + 1) * chunk)
        sh_ref[0, CONV_PAD + c * chunk:CONV_PAD + (c + 1) * chunk, :] = (
            u_ref[rs, 0:D_CH] * jax.nn.sigmoid(u_ref[rs, D_CH:2 * D_CH]))
    shifted_len = padded - SUBLANES
    for r in range(1, SUBLANES):
        for c0 in range(0, shifted_len, chunk):
            n = min(chunk, shifted_len - c0)
            sh_ref[r, c0:c0 + n, :] = sh_ref[0, c0 + r:c0 + r + n, :]
    first = CONV_PAD - CONV_WIDTH // 2
    for c in range(t_len // chunk):
        acc = None
        for j in range(CONV_WIDTH):
            off = first + j
            lo = c * chunk + off // SUBLANES * SUBLANES
            term = sh_ref[off % SUBLANES, lo:lo + chunk, :] * w_ref[j:j + 1, :]
            acc = term if acc is None else acc + term
        z = _layer_norm(acc + b_ref[...], g_ref[...], beta_ref[...])
        o_ref[c * chunk:(c + 1) * chunk, :] = jax.nn.silu(z).astype(o_ref.dtype)


def _conformer_conv(u, conv_w, conv_b, ln_g, ln_b):
    bsz, t_len, _ = u.shape
    vec = pl.BlockSpec((1, D_CH), lambda b: (0, 0))
    return pl.pallas_call(
        functools.partial(_conv_kernel, chunk=128),
        grid=(bsz,),
        in_specs=[pl.BlockSpec((None, t_len, 2 * D_CH), lambda b: (b, 0, 0)),
                  pl.BlockSpec(conv_w.shape, lambda b: (0, 0)), vec, vec, vec],
        out_specs=pl.BlockSpec((None, t_len, D_CH), lambda b: (b, 0, 0)),
        out_shape=jax.ShapeDtypeStruct((bsz, t_len, D_CH), BF16),
        scratch_shapes=[pltpu.VMEM((SUBLANES, t_len + 2 * CONV_PAD, D_CH), F32)],
        compiler_params=_params("parallel"),
        name="conformer_conv",
    )(u, conv_w, conv_b, ln_g, ln_b)


def _tail_kernel(a_ref, m_ref, wp_ref, x_ref, gate1_ref, g1_ref, b1_ref, sc_ref, sh_ref, gate_ref,
                 wi_ref, wo_ref, g_ref, b_ref, o_ref, acc_ref):
    wa = a_ref.shape[1]
    part = min(FFN_TM, x_ref.shape[0])
    n_parts = x_ref.shape[0] // part
    sub = min(ROW_TILE, part)
    n_chunks = FFN_HIDDEN // FFN_CHUNK

    def project(lo):
        rs = slice(lo, lo + sub)
        return _dot(a_ref[rs, :], wp_ref[0:wa, :]) + _dot(m_ref[rs, :], wp_ref[wa:, :])

    def mixer(p):
        y_next = project(p * part)
        for r in range(part // sub):
            y = y_next
            if r + 1 < part // sub:
                y_next = project(p * part + (r + 1) * sub)
            rs = slice(p * part + r * sub, p * part + (r + 1) * sub)
            z = ALPHA * x_ref[rs, :] + gate1_ref[...] * y
            o_ref[rs, :] = _layer_norm(z, g1_ref[...], b1_ref[...])

    def up(h, c):
        lo = c * FFN_CHUNK
        return (_dot(h, wi_ref[:, lo:lo + FFN_CHUNK]),
                _dot(h, wi_ref[:, FFN_HIDDEN + lo:FFN_HIDDEN + lo + FFN_CHUNK]))

    def ffn_start(p):
        h = (o_ref[p * part:(p + 1) * part, :] * (1.0 + sc_ref[...]) + sh_ref[...]).astype(BF16)
        return h, [up(h, c) for c in range(FFN_LOOKAHEAD)]

    def ffn_chunks(p, h, pending):
        for c in range(n_chunks):
            a, g = pending.pop(0)
            if c + FFN_LOOKAHEAD < n_chunks:
                pending.append(up(h, c + FFN_LOOKAHEAD))
            y = _dot((jax.nn.silu(g) * a).astype(BF16), wo_ref[c * FFN_CHUNK:(c + 1) * FFN_CHUNK, :])
            if c == 0:
                acc_ref[p % 2] = y
            else:
                acc_ref[p % 2] += y

    def ffn_finish(p):
        rs = slice(p * part, (p + 1) * part)
        z = ALPHA * o_ref[rs, :] + gate_ref[...] * acc_ref[p % 2]
        o_ref[rs, :] = _layer_norm(z, g_ref[...], b_ref[...])

    mixer(0)
    h, pending = ffn_start(0)
    for p in range(n_parts):
        if p + 1 < n_parts:
            mixer(p + 1)
        ffn_chunks(p, h, pending)
        if p + 1 < n_parts:
            h, pending = ffn_start(p + 1)
        ffn_finish(p)


def _tail(attn, mix, w_proj, x, mod, ln1_g, ln1_b, w_in, w_out, layer, ln2_g, ln2_b, *, tm, per_batch, ctx_row):
    bsz, rows, _ = x.shape
    vec = pl.BlockSpec((1, D_MODEL), lambda b, i: (0, 0))
    resident = dict(pipeline_mode=pl.Buffered(1))
    return pl.pallas_call(
        _tail_kernel,
        grid=(bsz, rows // tm),
        in_specs=[pl.BlockSpec((None, tm, attn.shape[2]), lambda b, i: (b, i, 0)),
                  pl.BlockSpec((None, tm, mix.shape[2]), lambda b, i: (b, i, 0)),
                  pl.BlockSpec(w_proj.shape, lambda b, i: (0, 0), **resident),
                  pl.BlockSpec((None, tm, D_MODEL), lambda b, i: (b, i, 0)),
                  _mod_spec(2, per_batch, ctx_row),
                  vec, vec,
                  _mod_spec(4, per_batch, ctx_row),
                  _mod_spec(3, per_batch, ctx_row),
                  _mod_spec(5, per_batch, ctx_row),
                  pl.BlockSpec((None,) + w_in.shape[1:], lambda b, i: (layer, 0, 0), **resident),
                  pl.BlockSpec((None,) + w_out.shape[1:], lambda b, i: (layer, 0, 0), **resident),
                  vec, vec],
        out_specs=pl.BlockSpec((None, tm, D_MODEL), lambda b, i: (b, i, 0)),
        out_shape=jax.ShapeDtypeStruct(x.shape, F32),
        scratch_shapes=[pltpu.VMEM((2, min(FFN_TM, tm), D_MODEL), F32)],
        compiler_params=_params("parallel", "parallel"),
        name="outproj_ffn",
    )(attn, mix, w_proj, x, mod, ln1_g, ln1_b, mod, mod, mod, w_in, w_out, ln2_g, ln2_b)


def _rope_tables(n_tokens):
    rows = n_tokens // GRID_W
    row = jnp.repeat(jnp.arange(rows, dtype=F32), GRID_W)
    col = jnp.tile(jnp.arange(GRID_W, dtype=F32), rows)
    axis_dim = HEAD_DIM // 2
    freqs = ROPE_THETA ** (-jnp.arange(0, axis_dim, 2, dtype=F32) / axis_dim)
    ang_r, ang_c = row[:, None] * freqs, col[:, None] * freqs
    cos = jnp.concatenate([jnp.cos(ang_r)] * 2 + [jnp.cos(ang_c)] * 2, axis=-1)
    sin = jnp.concatenate([-jnp.sin(ang_r), jnp.sin(ang_r), -jnp.sin(ang_c), jnp.sin(ang_c)], axis=-1)
    return jnp.tile(cos, (1, 2)), jnp.tile(sin, (1, 2))


def _pair_heads_cols(w):
    return w.reshape(w.shape[0], 2, 2, 3, HEAD_DIM).transpose(0, 1, 3, 2, 4).reshape(w.shape[0], A_WIDTH)


def _pair_heads_rows(w):
    return w.reshape(2, 2, 3, HEAD_DIM, w.shape[1]).transpose(0, 2, 1, 3, 4).reshape(A_WIDTH, w.shape[1])


def _block_diag(w_pool):
    out = jnp.zeros((B_WIDTH, B_WIDTH), w_pool.dtype)
    for g in range(B_GROUPS):
        sl = slice(g * B_GROUP_DIM, (g + 1) * B_GROUP_DIM)
        out = out.at[sl, sl].set(w_pool[g])
    return out


def kernel(x, c, ctx, c_ctx, ab_w_in, ab_q_gain, ab_k_gain, ab_w_pool, ab_pool_scale, ab_w_out,
           cd_w_in, cd_lambda_q1, cd_lambda_k1, cd_lambda_q2, cd_lambda_k2, cd_subln_gain,
           cd_conv_w, cd_conv_b, cd_conv_ln_g, cd_conv_ln_b, cd_w_out,
           ada_w, ada_b, ln1_g, ln1_b, ln2_g, ln2_b, ffn_w_in, ffn_w_out):
    bsz, t_len, _ = x.shape
    n_ctx = ctx.shape[1]
    ctx_row = bsz
    mod_rows = -(-(bsz + 1) // 8) * 8
    cc = jnp.zeros((mod_rows, D_MODEL), F32).at[:bsz].set(c).at[bsz].set(c_ctx)
    mods = _modulation(cc, ada_w, ada_b)
    rope = _rope_tables(t_len)
    row = lambda v: v.reshape(1, -1)
    head_of = jnp.arange(MXU_DIM) // HEAD_DIM
    gmat = jnp.where(head_of[:, None] == head_of[None, :], 1.0 / HEAD_DIM, 0.0).astype(BF16)
    w_ffn_in, w_ffn_out = ffn_w_in.astype(BF16), ffn_w_out.astype(BF16)

    l, i = 0, 0
    v0 = A_WIDTH + A_KV_WIDTH
    w_raw, wo_raw = ab_w_in[i].astype(BF16), ab_w_out[i].astype(BF16)
    w_in = jnp.concatenate([_pair_heads_cols(w_raw[:, :A_WIDTH]), w_raw[:, A_WIDTH:]], axis=1)
    w_out = jnp.concatenate([_pair_heads_rows(wo_raw[:A_WIDTH]), wo_raw[A_WIDTH:]], axis=0)
    gain = jnp.concatenate([jnp.tile(ab_q_gain[i], A_WIDTH // HEAD_DIM),
                            jnp.tile(ab_k_gain[i], A_KV_WIDTH // HEAD_DIM)]).reshape(1, -1)
    splits = [Split(0, A_WIDTH, BF16, Q_SCALE, True, True),
              Split(A_WIDTH, A_KV_WIDTH, BF16, 1.0, True, True),
              Split(v0 + A_KV_WIDTH, B_WIDTH, F32, 1.0, False, False),
              Split(v0, A_KV_WIDTH, BF16, 1.0, False, False, transposed=True)]
    w_bd = _block_diag(ab_w_pool[i]).astype(BF16)
    pool_scale = row(ab_pool_scale[i])
    g1, b1, g2, b2 = row(ln1_g[l]), row(ln1_b[l]), row(ln2_g[l]), row(ln2_b[l])

    q, k, u, vt = _inproj(x, mods[l], w_in, splits, tm=PROJ_TM, per_batch=True, ctx_row=ctx_row,
                          gain=gain, gmat=gmat, rope=rope)
    qc, kc, uc, vtc = _inproj(ctx, mods[l], w_in, splits, tm=n_ctx, per_batch=False, ctx_row=ctx_row,
                              gain=gain, gmat=gmat)
    o = _attn_gqa(q, [kc, k], [vtc, vt], tq=GQA_TQ)
    oc = _attn_gqa(qc, [kc], [vtc], tq=n_ctx)
    mix = _pool(u, w_bd, pool_scale)
    mixc = _pool(uc, w_bd, pool_scale)
    x = _tail(o, mix, w_out, x, mods[l], g1, b1, w_ffn_in, w_ffn_out, l, g2, b2,
              tm=TAIL_TM, per_batch=True, ctx_row=ctx_row)
    flat = lambda t: t.reshape(1, bsz * n_ctx, t.shape[-1])
    xc = _tail(flat(oc), flat(mixc), w_out, flat(ctx), mods[l], g1, b1, w_ffn_in, w_ffn_out, l, g2, b2,
               tm=TAIL_TM, per_batch=False, ctx_row=ctx_row).reshape(bsz, n_ctx, D_MODEL)

    l, i = 1, 0
    lam_init = 0.8 - 0.6 * math.exp(-0.3 * l)
    w_in = cd_w_in[i].astype(BF16)
    w_out = cd_w_out[i].astype(BF16)
    v_split = Split(2 * C_WIDTH, C_WIDTH, BF16, 1.0, False, False, transposed=True)
    splits = [Split(0, C_WIDTH, BF16, Q_SCALE, False, True),
              Split(C_WIDTH, C_WIDTH, BF16, 1.0, False, True),
              Split(3 * C_WIDTH, 2 * D_CH, F32, 1.0, False, False),
              v_split]
    ctx_splits = [Split(C_WIDTH, C_WIDTH, BF16, 1.0, False, False), v_split]
    lam_vecs = jnp.stack([cd_lambda_q1[i], cd_lambda_k1[i], cd_lambda_q2[i], cd_lambda_k2[i]])
    sub_gain = row(cd_subln_gain[i])
    g1, b1, g2, b2 = row(ln1_g[l]), row(ln1_b[l]), row(ln2_g[l]), row(ln2_b[l])

    q, k, u, vt = _inproj(x, mods[l], w_in, splits, tm=PROJ_TM, per_batch=True, ctx_row=ctx_row, rope=rope)
    kc, vtc = _inproj(xc, mods[l], w_in, ctx_splits, tm=n_ctx, per_batch=False, ctx_row=ctx_row)
    o = _attn_diff(lam_vecs, sub_gain, q, [kc, k], [vtc, vt], tq=DIFF_TQ, lam_init=lam_init)
    conv = _conformer_conv(u, cd_conv_w[i], row(cd_conv_b[i]), row(cd_conv_ln_g[i]), row(cd_conv_ln_b[i]))
    return _tail(o, conv, w_out, x, mods[l], g1, b1, w_ffn_in, w_ffn_out, l, g2, b2,
                 tm=TAIL_TM, per_batch=True, ctx_row=ctx_row)
```

```python
import functools
import math
from typing import NamedTuple

import jax
import jax.numpy as jnp
from jax import lax
from jax.experimental import pallas as pl
from jax.experimental.pallas import tpu as pltpu

F32 = jnp.float32
BF16 = jnp.bfloat16

D_MODEL = 1024
DEPTH = 2
GRID_W = 64
HEAD_DIM = 64
ROPE_THETA = 10000.0
EPS = 1e-6

B_WIDTH = D_MODEL // 4
B_GROUPS = 4
B_GROUP_DIM = B_WIDTH // B_GROUPS
POOL_WINDOWS = (2, 4, 8, 16)
A_WIDTH = D_MODEL - B_WIDTH
A_KV_WIDTH = 4 * HEAD_DIM
D_CH = D_MODEL // 4
C_WIDTH = D_MODEL - D_CH
C_V_DIM = 2 * HEAD_DIM
CONV_WIDTH = 31
FFN_HIDDEN = -(-8 * D_MODEL // (3 * 256)) * 256
ALPHA = (2.0 * DEPTH) ** 0.25

LANES = 128
SUBLANES = 8
BF16_SUBLANES = 16
Q_SCALE = HEAD_DIM ** -0.5 * math.log2(math.e)
MXU_DIM = 256
VMEM_LIMIT = 48 * 1024 * 1024

POOL_PAD = 8
CONV_PAD = 16
FFN_CHUNK = 256
FFN_LOOKAHEAD = 1
KEY_CHUNK = MXU_DIM
QK_LOOKAHEAD = 3
ROW_TILE = MXU_DIM
PROJ_TM = 1024
FFN_TM = 512
TAIL_TM = 1024
Q_TILE = MXU_DIM
GQA_TQ = 1024
DIFF_TQ = 2048


def _params(*semantics):
    return pltpu.CompilerParams(dimension_semantics=semantics, vmem_limit_bytes=VMEM_LIMIT)


def _layer_norm(z, g, b):
    mu = jnp.mean(z, axis=-1, keepdims=True)
    zc = z - mu
    var = jnp.mean(zc * zc, axis=-1, keepdims=True)
    return zc * lax.rsqrt(var + EPS) * g + b


def _dot(a, b):
    return jnp.dot(a, b, preferred_element_type=F32)


def _dot_nt(a, b):
    return lax.dot_general(a, b, (((1,), (1,)), ((), ())), preferred_element_type=F32)


def _mod_kernel(c_ref, w_ref, b_ref, o_ref):
    s = jax.nn.silu(c_ref[...])
    w = w_ref[...]
    s_hi, w_hi = s.astype(BF16), w.astype(BF16)
    s_lo = (s - s_hi.astype(F32)).astype(BF16)
    w_lo = (w - w_hi.astype(F32)).astype(BF16)
    o_ref[...] = _dot(s_hi, w_hi) + _dot(s_lo, w_hi) + _dot(s_hi, w_lo) + b_ref[...]


def _modulation(cc, ada_w, ada_b):
    rows = cc.shape[0]
    tn = 1536
    out = pl.pallas_call(
        _mod_kernel,
        grid=(DEPTH, 6 * D_MODEL // tn),
        in_specs=[
            pl.BlockSpec((rows, D_MODEL), lambda l, j: (0, 0)),
            pl.BlockSpec((None, D_MODEL, tn), lambda l, j: (l, 0, j)),
            pl.BlockSpec((None, 1, tn), lambda l, j: (l, 0, j)),
        ],
        out_specs=pl.BlockSpec((None, rows, tn), lambda l, j: (l, 0, j)),
        out_shape=jax.ShapeDtypeStruct((DEPTH, rows, 6 * D_MODEL), F32),
        compiler_params=_params("parallel", "parallel"),
        name="adaln_mod",
    )(cc, ada_w, ada_b.reshape(DEPTH, 1, 6 * D_MODEL))
    return out.reshape(DEPTH, rows, 1, 6 * D_MODEL)


def _mod_spec(chunk, per_batch, ctx_row):
    if per_batch:
        return pl.BlockSpec((None, 1, D_MODEL), lambda b, *_: (b, 0, chunk))
    return pl.BlockSpec((None, 1, D_MODEL), lambda b, *_: (ctx_row, 0, chunk))


class Split(NamedTuple):
    start: int
    width: int
    dtype: object
    scale: float
    norm: bool
    rope: bool
    transposed: bool = False
    head_major: bool = False


def _group_rms_norm(p, gain, gmat):
    ss = p * p
    hi = ss.astype(BF16)
    lo = (ss - hi.astype(F32)).astype(BF16)
    blocks = []
    for j in range(p.shape[1] // MXU_DIM):
        sl = slice(j * MXU_DIM, (j + 1) * MXU_DIM)
        blocks.append(_dot(hi[:, sl], gmat) + _dot(lo[:, sl], gmat))
    ms = blocks[0] if len(blocks) == 1 else jnp.concatenate(blocks, axis=1)
    return p * lax.rsqrt(ms + EPS) * gain


def _rope(p, cos, sin):
    tm = p.shape[0]
    lane = lax.broadcasted_iota(jnp.int32, (tm, LANES), 1)
    first = (lane & 16) == 0
    outs = []
    for j in range(p.shape[1] // LANES):
        xb = p[:, j * LANES:(j + 1) * LANES]
        partner = jnp.where(first, pltpu.roll(xb, LANES - 16, 1), pltpu.roll(xb, 16, 1))
        outs.append(xb * cos + partner * sin)
    return jnp.concatenate(outs, axis=1)


def _inproj_kernel(x_ref, sc_ref, sh_ref, w_ref, *rest, splits, use_norm, use_rope):
    idx = 0
    if use_norm:
        gain_ref, gmat_ref = rest[0], rest[1]
        idx = 2
    if use_rope:
        cos_ref, sin_ref = rest[idx], rest[idx + 1]
        idx += 2
    out_refs = rest[idx:]
    sub = min(ROW_TILE, x_ref.shape[0])
    n_sub = x_ref.shape[0] // sub

    def project(r):
        rs = slice(r * sub, (r + 1) * sub)
        h = (x_ref[rs, :] * (1.0 + sc_ref[...]) + sh_ref[...]).astype(BF16)
        return [_dot(h, w_ref[:, sp.start:sp.start + sp.width]) for sp in splits]

    def finish(r, ps):
        rs = slice(r * sub, (r + 1) * sub)
        gain_off = 0
        for sp, p, o_ref in zip(splits, ps, out_refs):
            if sp.norm:
                p = _group_rms_norm(p, gain_ref[:, gain_off:gain_off + sp.width], gmat_ref[...])
                gain_off += sp.width
            if sp.rope and use_rope:
                p = _rope(p, cos_ref[rs, :], sin_ref[rs, :])
            if sp.scale != 1.0:
                p = p * sp.scale
            if sp.transposed:
                o_ref[:, rs] = p.T.astype(o_ref.dtype)
            elif sp.head_major:
                for hd in range(sp.width // LANES):
                    o_ref[hd, rs, :] = p[:, hd * LANES:(hd + 1) * LANES].astype(o_ref.dtype)
            else:
                o_ref[rs, :] = p.astype(o_ref.dtype)

    ps_next = project(0)
    for r in range(n_sub):
        ps = ps_next
        if r + 1 < n_sub:
            ps_next = project(r + 1)
        finish(r, ps)


def _inproj(x, mod, w, splits, *, tm, per_batch, ctx_row, gain=None, gmat=None, rope=None):
    bsz, rows, _ = x.shape
    n_cols = w.shape[1]
    use_norm = gain is not None
    use_rope = rope is not None
    in_specs = [
        pl.BlockSpec((None, tm, D_MODEL), lambda b, i: (b, i, 0)),
        _mod_spec(1, per_batch, ctx_row),
        _mod_spec(0, per_batch, ctx_row),
        pl.BlockSpec((D_MODEL, n_cols), lambda b, i: (0, 0)),
    ]
    args = [x, mod, mod, w]
    if use_norm:
        in_specs += [pl.BlockSpec(gain.shape, lambda b, i: (0, 0)),
                     pl.BlockSpec(gmat.shape, lambda b, i: (0, 0))]
        args += [gain, gmat]
    if use_rope:
        in_specs += [pl.BlockSpec((tm, LANES), lambda b, i: (i, 0))] * 2
        args += [rope[0], rope[1]]
    def out_layout(sp):
        if sp.transposed:
            return (bsz, sp.width, rows), pl.BlockSpec((None, sp.width, tm), lambda b, i: (b, 0, i))
        if sp.head_major:
            heads = sp.width // LANES
            return (bsz, heads, rows, LANES), pl.BlockSpec((None, heads, tm, LANES), lambda b, i: (b, 0, i, 0))
        return (bsz, rows, sp.width), pl.BlockSpec((None, tm, sp.width), lambda b, i: (b, i, 0))

    out_specs = [out_layout(sp)[1] for sp in splits]
    out_shape = [jax.ShapeDtypeStruct(out_layout(sp)[0], sp.dtype) for sp in splits]
    return pl.pallas_call(
        functools.partial(_inproj_kernel, splits=tuple(splits), use_norm=use_norm, use_rope=use_rope),
        grid=(bsz, rows // tm),
        in_specs=in_specs,
        out_specs=out_specs,
        out_shape=out_shape,
        compiler_params=_params("parallel", "parallel"),
        name="inproj",
    )(*args)


def _head_qts(qp):
    qt = qp.astype(F32).T
    zeros = jnp.zeros((HEAD_DIM, qp.shape[0]), BF16)
    return [jnp.concatenate([qt[:HEAD_DIM].astype(BF16), zeros], axis=0),
            jnp.concatenate([zeros, qt[HEAD_DIM:].astype(BF16)], axis=0)]


def _softmax_pv_step(state, s, vt):
    cm = jnp.max(s, axis=0, keepdims=True)
    if state is None:
        return cm, _dot(vt, jnp.exp2(s - cm).astype(BF16))
    m, o = state
    m_new = jnp.maximum(m, cm)
    return m_new, jnp.exp2(m - m_new) * o + _dot(vt, jnp.exp2(s - m_new).astype(BF16))


def _attend(groups, k_refs, vt_refs, finish):
    chunks = [(k_ref, vt_ref, slice(c * KEY_CHUNK, (c + 1) * KEY_CHUNK))
              for k_ref, vt_ref in zip(k_refs, vt_refs) for c in range(k_ref.shape[0] // KEY_CHUNK)]
    items = [(g, j) for g in range(len(groups)) for j in range(len(chunks))]

    def scores(item):
        g, j = item
        k_ref, _, ks = chunks[j]
        kc = k_ref[ks, :]
        return [_dot(kc, qt) for qt, _ in groups[g]]

    ones = jnp.ones((BF16_SUBLANES, KEY_CHUNK), BF16)
    pending = [scores(item) for item in items[:QK_LOOKAHEAD]]
    for t, (g, j) in enumerate(items):
        s_cur = pending.pop(0)
        if t + QK_LOOKAHEAD < len(items):
            pending.append(scores(items[t + QK_LOOKAHEAD]))
        if j == 0:
            state = [None] * len(groups[g])
        _, vt_ref, ks = chunks[j]
        state = [_softmax_pv_step(st, s, jnp.concatenate([vt_ref[v_rows, ks], ones], axis=0))
                 for st, s, (_, v_rows) in zip(state, s_cur, groups[g])]
        if j == len(chunks) - 1:
            finish(g, [(o[:-BF16_SUBLANES], o[-BF16_SUBLANES:-BF16_SUBLANES + 1]) for _, o in state])


def _attn_gqa_kernel(q_ref, *refs, n_parts):
    k_refs = refs[:n_parts]
    vt_refs = refs[n_parts:2 * n_parts]
    o_ref = refs[2 * n_parts]
    groups, places = [], []
    for r in range(q_ref.shape[0] // Q_TILE):
        for pr in range(q_ref.shape[1] // LANES):
            place = (slice(r * Q_TILE, (r + 1) * Q_TILE), slice(pr * LANES, (pr + 1) * LANES))
            groups.append([(qt, slice(half * HEAD_DIM, (half + 1) * HEAD_DIM))
                           for half, qt in enumerate(_head_qts(q_ref[place]))])
            places.append(place)

    def finish(g, pair):
        o_pair = jnp.concatenate([o / l for o, l in pair], axis=0)
        o_ref[places[g]] = o_pair.T.astype(o_ref.dtype)

    _attend(groups, k_refs, vt_refs, finish)


def _attn_gqa(q, ks, vts, *, tq):
    bsz, rows, _ = q.shape
    n_parts = len(ks)
    qw = A_WIDTH // 2
    k_specs = [pl.BlockSpec((None, k.shape[1], LANES), lambda b, p, i: (b, 0, p)) for k in ks]
    vt_specs = [pl.BlockSpec((None, LANES, vt.shape[2]), lambda b, p, i: (b, p, 0)) for vt in vts]
    return pl.pallas_call(
        functools.partial(_attn_gqa_kernel, n_parts=n_parts),
        grid=(bsz, 2, rows // tq),
        in_specs=[pl.BlockSpec((None, tq, qw), lambda b, p, i: (b, i, p))] + k_specs + vt_specs,
        out_specs=pl.BlockSpec((None, tq, qw), lambda b, p, i: (b, i, p)),
        out_shape=jax.ShapeDtypeStruct((bsz, rows, A_WIDTH), BF16),
        compiler_params=_params("parallel", "parallel", "parallel"),
        name="attn_gqa",
    )(q, *ks, *vts)


def _attn_diff_kernel(lam_ref, gain_ref, q_ref, *refs, n_parts, lam_init):
    k_refs = refs[:n_parts]
    vt_refs = refs[n_parts:2 * n_parts]
    o_ref = refs[2 * n_parts]
    lv = lam_ref[...]
    lam = (jnp.exp(jnp.sum(lv[0:1] * lv[1:2], axis=1, keepdims=True))
           - jnp.exp(jnp.sum(lv[2:3] * lv[3:4], axis=1, keepdims=True)) + lam_init)
    rows = slice(0, C_V_DIM)
    groups = []
    for r in range(q_ref.shape[0] // Q_TILE):
        groups.append([(qt, rows) for qt in _head_qts(q_ref[r * Q_TILE:(r + 1) * Q_TILE, :])])

    def finish(g, comps):
        (o0, l0), (o1, l1) = comps
        o = o0 * (1.0 / l0) - o1 * (lam / l1)
        ms = jnp.mean(o * o, axis=0, keepdims=True)
        o = (o * lax.rsqrt(ms + EPS)).T
        o_ref[g * Q_TILE:(g + 1) * Q_TILE, :] = (o * gain_ref[...] * (1.0 - lam_init)).astype(o_ref.dtype)

    _attend(groups, k_refs, vt_refs, finish)


def _attn_diff(lam_vecs, gain, q, ks, vts, *, tq, lam_init):
    bsz, heads, rows, _ = q.shape
    n_parts = len(ks)
    k_specs = [pl.BlockSpec((None, None, k.shape[2], LANES), lambda b, h, i: (b, h, 0, 0)) for k in ks]
    vt_specs = [pl.BlockSpec((None, C_V_DIM, vt.shape[2]), lambda b, h, i: (b, h, 0)) for vt in vts]
    return pl.pallas_call(
        functools.partial(_attn_diff_kernel, n_parts=n_parts, lam_init=lam_init),
        grid=(bsz, heads, rows // tq),
        in_specs=[pl.BlockSpec(lam_vecs.shape, lambda b, h, i: (0, 0)),
                  pl.BlockSpec(gain.shape, lambda b, h, i: (0, 0)),
                  pl.BlockSpec((None, None, tq, LANES), lambda b, h, i: (b, h, i, 0))] + k_specs + vt_specs,
        out_specs=pl.BlockSpec((None, None, tq, LANES), lambda b, h, i: (b, h, i, 0)),
        out_shape=jax.ShapeDtypeStruct((bsz, heads, rows, LANES), BF16),
        compiler_params=_params("parallel", "parallel", "parallel"),
        name="attn_diff",
    )(lam_vecs, gain, q, *ks, *vts)


def _pool_kernel(u_ref, w_ref, ps_ref, o_ref, sh_ref, *, chunk):
    t_len = u_ref.shape[0]
    padded = t_len + 2 * POOL_PAD
    zeros = jnp.zeros((POOL_PAD, B_WIDTH), F32)
    sh_ref[0, 0:POOL_PAD, :] = zeros
    sh_ref[0, POOL_PAD + t_len:padded, :] = zeros
    sh_ref[0, POOL_PAD:POOL_PAD + t_len, :] = u_ref[...]
    shifted_len = padded - SUBLANES
    per_col = LANES // B_GROUP_DIM
    col_windows = [POOL_WINDOWS[c * per_col:(c + 1) * per_col] for c in range(B_WIDTH // LANES)]

    def offsets(w):
        return range(-(w // 2), w - w // 2)

    for col, ws in enumerate(col_windows):
        cs = slice(col * LANES, (col + 1) * LANES)
        for r in sorted({(POOL_PAD + j) % SUBLANES for j in offsets(ws[-1])} - {0}):
            for c0 in range(0, shifted_len, chunk):
                n = min(chunk, shifted_len - c0)
                sh_ref[r, c0:c0 + n, cs] = sh_ref[0, c0 + r:c0 + r + n, cs]

    grp = lax.broadcasted_iota(jnp.int32, (chunk, LANES), 1) // B_GROUP_DIM
    for c in range(t_len // chunk):
        r0 = c * chunk
        t = lax.broadcasted_iota(jnp.int32, (chunk, LANES), 0) + r0
        cols = []
        for col, ws in enumerate(col_windows):
            cs = slice(col * LANES, (col + 1) * LANES)

            def win(j):
                off = POOL_PAD + j
                lo = r0 + off // SUBLANES * SUBLANES
                return sh_ref[off % SUBLANES, lo:lo + chunk, cs]

            acc, done, s, half, tail = None, [], None, None, None
            for k, w in enumerate(ws):
                new = [win(j) for j in offsets(w) if j not in done]
                done += list(offsets(w))
                acc = functools.reduce(jnp.add, ([] if acc is None else [acc]) + new)
                s = acc if s is None else jnp.where(grp == k, acc, s)
                half = w // 2 if half is None else jnp.where(grp == k, w // 2, half)
                tail = w - 1 - w // 2 if tail is None else jnp.where(grp == k, w - 1 - w // 2, tail)
            cnt = (jnp.minimum(t + tail + 1, t_len) - jnp.maximum(t - half, 0)).astype(F32)
            cols.append(s / cnt - win(0))
        pooled = jnp.concatenate(cols, axis=1)
        mixed = _dot(pooled.astype(BF16), w_ref[...]) * ps_ref[...]
        o_ref[r0:r0 + chunk, :] = mixed.astype(o_ref.dtype)


def _pool(u, w_bd, pool_scale):
    bsz, t_len, _ = u.shape
    return pl.pallas_call(
        functools.partial(_pool_kernel, chunk=min(t_len, 256)),
        grid=(bsz,),
        in_specs=[pl.BlockSpec((None, t_len, B_WIDTH), lambda b: (b, 0, 0)),
                  pl.BlockSpec(w_bd.shape, lambda b: (0, 0)),
                  pl.BlockSpec(pool_scale.shape, lambda b: (0, 0))],
        out_specs=pl.BlockSpec((None, t_len, B_WIDTH), lambda b: (b, 0, 0)),
        out_shape=jax.ShapeDtypeStruct((bsz, t_len, B_WIDTH), BF16),
        scratch_shapes=[pltpu.VMEM((SUBLANES, t_len + 2 * POOL_PAD, B_WIDTH), F32)],
        compiler_params=_params("parallel"),
        name="pool_mixer",
    )(u, w_bd, pool_scale)


def _conv_kernel(u_ref, w_ref, b_ref, g_ref, beta_ref, o_ref, sh_ref, *, chunk):
    t_len = u_ref.shape[0]
    padded = t_len + 2 * CONV_PAD
    zeros = jnp.zeros((CONV_PAD, D_CH), F32)
    sh_ref[0, 0:CONV_PAD, :] = zeros
    sh_ref[0, CONV_PAD + t_len:padded, :] = zeros
    for c in range(t_len // chunk):
        rs = slice(c * chunk, (c + 1) * chunk)
        sh_ref[0, CONV_PAD + c * chunk:CONV_PAD + (c + 1) * chunk, :] = (
            u_ref[rs, 0:D_CH] * jax.nn.sigmoid(u_ref[rs, D_CH:2 * D_CH]))
    shifted_len = padded - SUBLANES
    for r in range(1, SUBLANES):
        for c0 in range(0, shifted_len, chunk):
            n = min(chunk, shifted_len - c0)
            sh_ref[r, c0:c0 + n, :] = sh_ref[0, c0 + r:c0 + r + n, :]
    first = CONV_PAD - CONV_WIDTH // 2
    for c in range(t_len // chunk):
        acc = None
        for j in range(CONV_WIDTH):
            off = first + j
            lo = c * chunk + off // SUBLANES * SUBLANES
            term = sh_ref[off % SUBLANES, lo:lo + chunk, :] * w_ref[j:j + 1, :]
            acc = term if acc is None else acc + term
        z = _layer_norm(acc + b_ref[...], g_ref[...], beta_ref[...])
        o_ref[c * chunk:(c + 1) * chunk, :] = jax.nn.silu(z).astype(o_ref.dtype)


def _conformer_conv(u, conv_w, conv_b, ln_g, ln_b):
    bsz, t_len, _ = u.shape
    vec = pl.BlockSpec((1, D_CH), lambda b: (0, 0))
    return pl.pallas_call(
        functools.partial(_conv_kernel, chunk=128),
        grid=(bsz,),
        in_specs=[pl.BlockSpec((None, t_len, 2 * D_CH), lambda b: (b, 0, 0)),
                  pl.BlockSpec(conv_w.shape, lambda b: (0, 0)), vec, vec, vec],
        out_specs=pl.BlockSpec((None, t_len, D_CH), lambda b: (b, 0, 0)),
        out_shape=jax.ShapeDtypeStruct((bsz, t_len, D_CH), BF16),
        scratch_shapes=[pltpu.VMEM((SUBLANES, t_len + 2 * CONV_PAD, D_CH), F32)],
        compiler_params=_params("parallel"),
        name="conformer_conv",
    )(u, conv_w, conv_b, ln_g, ln_b)


def _tail_kernel(a_ref, m_ref, wp_ref, x_ref, gate1_ref, g1_ref, b1_ref, sc_ref, sh_ref, gate_ref,
                 wi_ref, wo_ref, g_ref, b_ref, o_ref, acc_ref):
    head_major = len(a_ref.shape) == 3
    wa = a_ref.shape[0] * a_ref.shape[2] if head_major else a_ref.shape[1]
    part = min(FFN_TM, x_ref.shape[0])
    n_parts = x_ref.shape[0] // part
    sub = min(ROW_TILE, part)
    n_chunks = FFN_HIDDEN // FFN_CHUNK

    def project(lo):
        rs = slice(lo, lo + sub)
        if head_major:
            a = jnp.concatenate([a_ref[hd, rs, :] for hd in range(a_ref.shape[0])], axis=1)
        else:
            a = a_ref[rs, :]
        return _dot(a, wp_ref[0:wa, :]) + _dot(m_ref[rs, :], wp_ref[wa:, :])

    def mixer(p):
        y_next = project(p * part)
        for r in range(part // sub):
            y = y_next
            if r + 1 < part // sub:
                y_next = project(p * part + (r + 1) * sub)
            rs = slice(p * part + r * sub, p * part + (r + 1) * sub)
            z = ALPHA * x_ref[rs, :] + gate1_ref[...] * y
            o_ref[rs, :] = _layer_norm(z, g1_ref[...], b1_ref[...])

    def up(h, c):
        lo = c * FFN_CHUNK
        return (_dot(h, wi_ref[:, lo:lo + FFN_CHUNK]),
                _dot(h, wi_ref[:, FFN_HIDDEN + lo:FFN_HIDDEN + lo + FFN_CHUNK]))

    def ffn_start(p):
        h = (o_ref[p * part:(p + 1) * part, :] * (1.0 + sc_ref[...]) + sh_ref[...]).astype(BF16)
        return h, [up(h, c) for c in range(FFN_LOOKAHEAD)]

    def ffn_chunks(p, h, pending):
        for c in range(n_chunks):
            a, g = pending.pop(0)
            if c + FFN_LOOKAHEAD < n_chunks:
                pending.append(up(h, c + FFN_LOOKAHEAD))
            y = _dot((jax.nn.silu(g) * a).astype(BF16), wo_ref[c * FFN_CHUNK:(c + 1) * FFN_CHUNK, :])
            if c == 0:
                acc_ref[p % 2] = y
            else:
                acc_ref[p % 2] += y

    def ffn_finish(p):
        rs = slice(p * part, (p + 1) * part)
        z = ALPHA * o_ref[rs, :] + gate_ref[...] * acc_ref[p % 2]
        o_ref[rs, :] = _layer_norm(z, g_ref[...], b_ref[...])

    mixer(0)
    h, pending = ffn_start(0)
    for p in range(n_parts):
        if p + 1 < n_parts:
            mixer(p + 1)
        ffn_chunks(p, h, pending)
        if p + 1 < n_parts:
            h, pending = ffn_start(p + 1)
        ffn_finish(p)


def _tail(attn, mix, w_proj, x, mod, ln1_g, ln1_b, w_in, w_out, layer, ln2_g, ln2_b, *, tm, per_batch, ctx_row):
    bsz, rows, _ = x.shape
    vec = pl.BlockSpec((1, D_MODEL), lambda b, i: (0, 0))
    resident = dict(pipeline_mode=pl.Buffered(1))
    return pl.pallas_call(
        _tail_kernel,
        grid=(bsz, rows // tm),
        in_specs=[pl.BlockSpec((None, attn.shape[1], tm, LANES), lambda b, i: (b, 0, i, 0)) if attn.ndim == 4
                  else pl.BlockSpec((None, tm, attn.shape[2]), lambda b, i: (b, i, 0)),
                  pl.BlockSpec((None, tm, mix.shape[2]), lambda b, i: (b, i, 0)),
                  pl.BlockSpec(w_proj.shape, lambda b, i: (0, 0), **resident),
                  pl.BlockSpec((None, tm, D_MODEL), lambda b, i: (b, i, 0)),
                  _mod_spec(2, per_batch, ctx_row),
                  vec, vec,
                  _mod_spec(4, per_batch, ctx_row),
                  _mod_spec(3, per_batch, ctx_row),
                  _mod_spec(5, per_batch, ctx_row),
                  pl.BlockSpec((None,) + w_in.shape[1:], lambda b, i: (layer, 0, 0), **resident),
                  pl.BlockSpec((None,) + w_out.shape[1:], lambda b, i: (layer, 0, 0), **resident),
                  vec, vec],
        out_specs=pl.BlockSpec((None, tm, D_MODEL), lambda b, i: (b, i, 0)),
        out_shape=jax.ShapeDtypeStruct(x.shape, F32),
        scratch_shapes=[pltpu.VMEM((2, min(FFN_TM, tm), D_MODEL), F32)],
        compiler_params=_params("parallel", "parallel"),
        name="outproj_ffn",
    )(attn, mix, w_proj, x, mod, ln1_g, ln1_b, mod, mod, mod, w_in, w_out, ln2_g, ln2_b)


def _rope_tables(n_tokens):
    rows = n_tokens // GRID_W
    row = jnp.repeat(jnp.arange(rows, dtype=F32), GRID_W)
    col = jnp.tile(jnp.arange(GRID_W, dtype=F32), rows)
    axis_dim = HEAD_DIM // 2
    freqs = ROPE_THETA ** (-jnp.arange(0, axis_dim, 2, dtype=F32) / axis_dim)
    ang_r, ang_c = row[:, None] * freqs, col[:, None] * freqs
    cos = jnp.concatenate([jnp.cos(ang_r)] * 2 + [jnp.cos(ang_c)] * 2, axis=-1)
    sin = jnp.concatenate([-jnp.sin(ang_r), jnp.sin(ang_r), -jnp.sin(ang_c), jnp.sin(ang_c)], axis=-1)
    return jnp.tile(cos, (1, 2)), jnp.tile(sin, (1, 2))


def _pair_heads_cols(w):
    return w.reshape(w.shape[0], 2, 2, 3, HEAD_DIM).transpose(0, 1, 3, 2, 4).reshape(w.shape[0], A_WIDTH)


def _pair_heads_rows(w):
    return w.reshape(2, 2, 3, HEAD_DIM, w.shape[1]).transpose(0, 2, 1, 3, 4).reshape(A_WIDTH, w.shape[1])


def _block_diag(w_pool):
    out = jnp.zeros((B_WIDTH, B_WIDTH), w_pool.dtype)
    for g in range(B_GROUPS):
        sl = slice(g * B_GROUP_DIM, (g + 1) * B_GROUP_DIM)
        out = out.at[sl, sl].set(w_pool[g])
    return out


def kernel(x, c, ctx, c_ctx, ab_w_in, ab_q_gain, ab_k_gain, ab_w_pool, ab_pool_scale, ab_w_out,
           cd_w_in, cd_lambda_q1, cd_lambda_k1, cd_lambda_q2, cd_lambda_k2, cd_subln_gain,
           cd_conv_w, cd_conv_b, cd_conv_ln_g, cd_conv_ln_b, cd_w_out,
           ada_w, ada_b, ln1_g, ln1_b, ln2_g, ln2_b, ffn_w_in, ffn_w_out):
    bsz, t_len, _ = x.shape
    n_ctx = ctx.shape[1]
    ctx_row = bsz
    mod_rows = -(-(bsz + 1) // 8) * 8
    cc = jnp.zeros((mod_rows, D_MODEL), F32).at[:bsz].set(c).at[bsz].set(c_ctx)
    mods = _modulation(cc, ada_w, ada_b)
    rope = _rope_tables(t_len)
    row = lambda v: v.reshape(1, -1)
    head_of = jnp.arange(MXU_DIM) // HEAD_DIM
    gmat = jnp.where(head_of[:, None] == head_of[None, :], 1.0 / HEAD_DIM, 0.0).astype(BF16)
    w_ffn_in, w_ffn_out = ffn_w_in.astype(BF16), ffn_w_out.astype(BF16)

    l, i = 0, 0
    v0 = A_WIDTH + A_KV_WIDTH
    w_raw, wo_raw = ab_w_in[i].astype(BF16), ab_w_out[i].astype(BF16)
    w_in = jnp.concatenate([_pair_heads_cols(w_raw[:, :A_WIDTH]), w_raw[:, A_WIDTH:]], axis=1)
    w_out = jnp.concatenate([_pair_heads_rows(wo_raw[:A_WIDTH]), wo_raw[A_WIDTH:]], axis=0)
    gain = jnp.concatenate([jnp.tile(ab_q_gain[i], A_WIDTH // HEAD_DIM),
                            jnp.tile(ab_k_gain[i], A_KV_WIDTH // HEAD_DIM)]).reshape(1, -1)
    splits = [Split(0, A_WIDTH, BF16, Q_SCALE, True, True),
              Split(A_WIDTH, A_KV_WIDTH, BF16, 1.0, True, True),
              Split(v0 + A_KV_WIDTH, B_WIDTH, F32, 1.0, False, False),
              Split(v0, A_KV_WIDTH, BF16, 1.0, False, False, transposed=True)]
    w_bd = _block_diag(ab_w_pool[i]).astype(BF16)
    pool_scale = row(ab_pool_scale[i])
    g1, b1, g2, b2 = row(ln1_g[l]), row(ln1_b[l]), row(ln2_g[l]), row(ln2_b[l])

    q, k, u, vt = _inproj(x, mods[l], w_in, splits, tm=PROJ_TM, per_batch=True, ctx_row=ctx_row,
                          gain=gain, gmat=gmat, rope=rope)
    qc, kc, uc, vtc = _inproj(ctx, mods[l], w_in, splits, tm=n_ctx, per_batch=False, ctx_row=ctx_row,
                              gain=gain, gmat=gmat)
    o = _attn_gqa(q, [kc, k], [vtc, vt], tq=GQA_TQ)
    oc = _attn_gqa(qc, [kc], [vtc], tq=n_ctx)
    mix = _pool(u, w_bd, pool_scale)
    mixc = _pool(uc, w_bd, pool_scale)
    x = _tail(o, mix, w_out, x, mods[l], g1, b1, w_ffn_in, w_ffn_out, l, g2, b2,
              tm=TAIL_TM, per_batch=True, ctx_row=ctx_row)
    flat = lambda t: t.reshape(1, bsz * n_ctx, t.shape[-1])
    xc = _tail(flat(oc), flat(mixc), w_out, flat(ctx), mods[l], g1, b1, w_ffn_in, w_ffn_out, l, g2, b2,
               tm=TAIL_TM, per_batch=False, ctx_row=ctx_row).reshape(bsz, n_ctx, D_MODEL)

    l, i = 1, 0
    lam_init = 0.8 - 0.6 * math.exp(-0.3 * l)
    w_in = cd_w_in[i].astype(BF16)
    w_out = cd_w_out[i].astype(BF16)
    v_split = Split(2 * C_WIDTH, C_WIDTH, BF16, 1.0, False, False, transposed=True)
    splits = [Split(0, C_WIDTH, BF16, Q_SCALE, False, True, head_major=True),
              Split(C_WIDTH, C_WIDTH, BF16, 1.0, False, True, head_major=True),
              Split(3 * C_WIDTH, 2 * D_CH, F32, 1.0, False, False),
              v_split]
    ctx_splits = [Split(C_WIDTH, C_WIDTH, BF16, 1.0, False, False, head_major=True), v_split]
    lam_vecs = jnp.stack([cd_lambda_q1[i], cd_lambda_k1[i], cd_lambda_q2[i], cd_lambda_k2[i]])
    sub_gain = row(cd_subln_gain[i])
    g1, b1, g2, b2 = row(ln1_g[l]), row(ln1_b[l]), row(ln2_g[l]), row(ln2_b[l])

    q, k, u, vt = _inproj(x, mods[l], w_in, splits, tm=PROJ_TM, per_batch=True, ctx_row=ctx_row, rope=rope)
    kc, vtc = _inproj(xc, mods[l], w_in, ctx_splits, tm=n_ctx, per_batch=False, ctx_row=ctx_row)
    o = _attn_diff(lam_vecs, sub_gain, q, [kc, k], [vtc, vt], tq=DIFF_TQ, lam_init=lam_init)
    conv = _conformer_conv(u, cd_conv_w[i], row(cd_conv_b[i]), row(cd_conv_ln_g[i]), row(cd_conv_ln_b[i]))
    return _tail(o, conv, w_out, x, mods[l], g1, b1, w_ffn_in, w_ffn_out, l, g2, b2,
                 tm=TAIL_TM, per_batch=True, ctx_row=ctx_row)
```
